```python
import jax, jax.numpy as jnp
from jax import lax
import numpy as np

D_MODEL = 1024
BATCH = 32
SEQ = 2048
DEPTH = 1

MIX_WIDTH = D_MODEL
LRU_WIDTH = MIX_WIDTH // 2
LRU_HEADS = 8
LRU_HEAD_DIM = LRU_WIDTH // LRU_HEADS
LRU_CONV_K = 4
LRU_C = 8.0
CONV_WIDTH = MIX_WIDTH - LRU_WIDTH
CONV_HEADS = 8
CONV_HEAD_DIM = CONV_WIDTH // CONV_HEADS
CONV_K = 31
D_FF = ((8 * D_MODEL // 3 + 255) // 256) * 256
FFN_CONV_K = 3
ALPHA = (2 * DEPTH) ** 0.25
BETA = (8 * DEPTH) ** -0.25
LN_EPS = 1e-5

kernel_name = "hybrid_rglru_conformer_convglu_deepnorm_block"


def layer_norm(x, g, b):
    xf = x.astype(jnp.float32)
    mu = jnp.mean(xf, axis=-1, keepdims=True)
    var = jnp.mean(jnp.square(xf - mu), axis=-1, keepdims=True)
    y = (xf - mu) * lax.rsqrt(var + LN_EPS)
    return (y * g.astype(jnp.float32) + b.astype(jnp.float32)).astype(x.dtype)


def head_layer_norm(x, n_heads, g, b):
    bsz, seq, ch = x.shape
    xf = x.astype(jnp.float32).reshape(bsz, seq, n_heads, ch // n_heads)
    mu = jnp.mean(xf, axis=-1, keepdims=True)
    var = jnp.mean(jnp.square(xf - mu), axis=-1, keepdims=True)
    y = ((xf - mu) * lax.rsqrt(var + LN_EPS)).reshape(bsz, seq, ch)
    return (y * g.astype(jnp.float32) + b.astype(jnp.float32)).astype(x.dtype)


def causal_dwconv(x, w, b):
    k, ch = w.shape
    y = lax.conv_general_dilated(
        x, w[:, None, :].astype(x.dtype), window_strides=(1,),
        padding=[(k - 1, 0)], dimension_numbers=("NWC", "WIO", "NWC"),
        feature_group_count=ch)
    return y + b.astype(x.dtype)


def rg_lru(x, w_r, b_r, w_i, b_i, lam):
    bsz, seq, ch = x.shape
    xf = x.astype(jnp.float32)
    xh = xf.reshape(bsz, seq, LRU_HEADS, LRU_HEAD_DIM)
    gr = jnp.einsum("bshi,hij->bshj", xh, w_r.astype(jnp.float32)).reshape(bsz, seq, ch)
    gi = jnp.einsum("bshi,hij->bshj", xh, w_i.astype(jnp.float32)).reshape(bsz, seq, ch)
    r = jax.nn.sigmoid(gr + b_r.astype(jnp.float32))
    i = jax.nn.sigmoid(gi + b_i.astype(jnp.float32))
    log_a = -LRU_C * r * jax.nn.softplus(-lam.astype(jnp.float32))
    a = jnp.exp(log_a)
    mult = jnp.sqrt(-jnp.expm1(2.0 * log_a))
    u = mult * (i * xf)

    def combine(left, right):
        a_l, h_l = left
        a_r, h_r = right
        return a_l * a_r, a_r * h_l + h_r

    _, h = lax.associative_scan(combine, (a, u), axis=1)
    return h.astype(x.dtype)


def hybrid_mixer(u, w_in, lru_conv_w, lru_conv_b, lru_w_r, lru_b_r, lru_w_i, lru_b_i,
                 lru_lambda, conv_w, conv_b, conv_norm_g, conv_norm_b, w_out):
    proj = jnp.einsum("bsd,de->bse", u, w_in)
    xa, ga, vb, gb = jnp.split(
        proj, [LRU_WIDTH, 2 * LRU_WIDTH, 2 * LRU_WIDTH + CONV_WIDTH], axis=-1)
    xa = causal_dwconv(xa, lru_conv_w, lru_conv_b)
    ha = rg_lru(xa, lru_w_r, lru_b_r, lru_w_i, lru_b_i, lru_lambda)
    ya = jax.nn.gelu(ga) * ha
    vb = vb * jax.nn.sigmoid(gb)
    vb = causal_dwconv(vb, conv_w, conv_b)
    yb = jax.nn.silu(head_layer_norm(vb, CONV_HEADS, conv_norm_g, conv_norm_b))
    y = jnp.concatenate([ya, yb], axis=-1)
    return jnp.einsum("bse,ed->bsd", y, w_out)


def conv_glu_ffn(u, w_up, conv_w, conv_b, w_down):
    h = jnp.einsum("bsd,df->bsf", u, w_up)
    v, g = jnp.split(h, 2, axis=-1)
    g = causal_dwconv(g, conv_w, conv_b)
    return jnp.einsum("bsf,fd->bsd", jax.nn.silu(g) * v, w_down)


def setup_inputs(seed: int = 0) -> dict:
    key = jax.random.key(seed)
    ks = jax.random.split(key, 26)
    L, D = DEPTH, D_MODEL

    def nrm(k, shape, scale):
        return jax.random.normal(k, shape, jnp.float32) * scale

    a_pow = jax.random.uniform(ks[11], (L, LRU_WIDTH), jnp.float32, 0.9, 0.999)
    a_base = a_pow ** (1.0 / LRU_C)
    lru_lambda = jnp.log(a_base) - jnp.log1p(-a_base)
    return {
        "x": nrm(ks[0], (BATCH, SEQ, D), 1.0),
        "c": nrm(ks[1], (BATCH, D), 1.0),
        "w_ada": nrm(ks[2], (L, D, 6 * D), 0.1 * D ** -0.5),
        "b_ada": nrm(ks[3], (L, 6 * D), 0.01),
        "w_in": nrm(ks[4], (L, D, 2 * LRU_WIDTH + 2 * CONV_WIDTH), D ** -0.5),
        "lru_conv_w": nrm(ks[5], (L, LRU_CONV_K, LRU_WIDTH), LRU_CONV_K ** -0.5),
        "lru_conv_b": nrm(ks[6], (L, LRU_WIDTH), 0.01),
        "lru_w_r": nrm(ks[7], (L, LRU_HEADS, LRU_HEAD_DIM, LRU_HEAD_DIM), LRU_HEAD_DIM ** -0.5),
        "lru_b_r": nrm(ks[8], (L, LRU_WIDTH), 0.01),
        "lru_w_i": nrm(ks[9], (L, LRU_HEADS, LRU_HEAD_DIM, LRU_HEAD_DIM), LRU_HEAD_DIM ** -0.5),
        "lru_b_i": nrm(ks[10], (L, LRU_WIDTH), 0.01),
        "lru_lambda": lru_lambda,
        "conv_w": nrm(ks[12], (L, CONV_K, CONV_WIDTH), CONV_K ** -0.5),
        "conv_b": nrm(ks[13], (L, CONV_WIDTH), 0.01),
        "conv_norm_g": 1.0 + nrm(ks[14], (L, CONV_WIDTH), 0.01),
        "conv_norm_b": nrm(ks[15], (L, CONV_WIDTH), 0.01),
        "w_out": nrm(ks[16], (L, MIX_WIDTH, D), BETA * MIX_WIDTH ** -0.5),
        "ln1_g": 1.0 + nrm(ks[17], (L, D), 0.01),
        "ln1_b": nrm(ks[18], (L, D), 0.01),
        "ffn_w_up": nrm(ks[19], (L, D, 2 * D_FF), D ** -0.5),
        "ffn_conv_w": nrm(ks[20], (L, FFN_CONV_K, D_FF), FFN_CONV_K ** -0.5),
        "ffn_conv_b": nrm(ks[21], (L, D_FF), 0.01),
        "ffn_w_down": nrm(ks[22], (L, D_FF, D), BETA * D_FF ** -0.5),
        "ln2_g": 1.0 + nrm(ks[23], (L, D), 0.01),
        "ln2_b": nrm(ks[24], (L, D), 0.01),
    }


def reference(x, c, w_ada, b_ada, w_in, lru_conv_w, lru_conv_b, lru_w_r, lru_b_r,
              lru_w_i, lru_b_i, lru_lambda, conv_w, conv_b, conv_norm_g, conv_norm_b,
              w_out, ln1_g, ln1_b, ffn_w_up, ffn_conv_w, ffn_conv_b, ffn_w_down,
              ln2_g, ln2_b):
    c_act = jax.nn.silu(c)
    for l in range(DEPTH):
        mod = jnp.einsum("bd,de->be", c_act, w_ada[l]) + b_ada[l]
        sh1, sc1, gt1, sh2, sc2, gt2 = [m[:, None, :] for m in jnp.split(mod, 6, axis=-1)]
        u = x * (1.0 + sc1) + sh1
        y = hybrid_mixer(u, w_in[l], lru_conv_w[l], lru_conv_b[l], lru_w_r[l], lru_b_r[l],
                         lru_w_i[l], lru_b_i[l], lru_lambda[l], conv_w[l], conv_b[l],
                         conv_norm_g[l], conv_norm_b[l], w_out[l])
        x = layer_norm(ALPHA * x + (1.0 + gt1) * y, ln1_g[l], ln1_b[l])
        u = x * (1.0 + sc2) + sh2
        y = conv_glu_ffn(u, ffn_w_up[l], ffn_conv_w[l], ffn_conv_b[l], ffn_w_down[l])
        x = layer_norm(ALPHA * x + (1.0 + gt2) * y, ln2_g[l], ln2_b[l])
    return x
```

```python
import functools

import jax
import jax.numpy as jnp
from jax import lax
from jax.experimental import pallas as pl
from jax.experimental.pallas import tpu as pltpu

LRU_C = 8.0
LN_EPS = 1e-5
SUBLANES = 8
LRU_HALO = 8
CONV_HALO = 32
FFN_HALO = 8
VMEM_LIMIT_BYTES = 56 * 1024 * 1024
TIME_BLOCK = 256


def _dot(a, b):
    return jnp.dot(a, b, preferred_element_type=jnp.float32)


def _sigmoid(v):
    return 0.5 * jnp.tanh(0.5 * v) + 0.5


def _layer_norm_rows(z, g, b):
    mu = jnp.mean(z, axis=-1, keepdims=True)
    d = z - mu
    var = jnp.mean(d * d, axis=-1, keepdims=True)
    return d * lax.rsqrt(var + LN_EPS) * g + b


def _mod_kernel(c_ref, w_ref, b_ref, o_ref):
    c = c_ref[...]
    c_act = c * _sigmoid(c)
    o_ref[...] = _dot(c_act, w_ref[...]) + b_ref[...]


def _modulation(c, w_ada, b_ada):
    bsz, d = c.shape
    n = w_ada.shape[1]
    return pl.pallas_call(
        _mod_kernel,
        grid=(n // d,),
        in_specs=[
            pl.BlockSpec((bsz, d), lambda j: (0, 0)),
            pl.BlockSpec((d, d), lambda j: (0, j)),
            pl.BlockSpec((1, d), lambda j: (0, j)),
        ],
        out_specs=pl.BlockSpec((bsz, d), lambda j: (0, j)),
        out_shape=jax.ShapeDtypeStruct((bsz, n), jnp.float32),
        compiler_params=pltpu.CompilerParams(
            dimension_semantics=("arbitrary",), vmem_limit_bytes=VMEM_LIMIT_BYTES),
        name="adaln_modulation",
    )(c, w_ada, b_ada.reshape(1, n))


def _mixer_kernel(x_ref, mod_ref, w_in_ref, cw4_ref, cb4_ref, wg_ref, br_ref, bi_ref, lam_ref,
                  cw31_ref, cb31_ref, ng_ref, nb_ref, mavg_ref, w_out_ref, lng_ref, lnb_ref,
                  o_ref,
                  u_scr, xa_buf, ga_scr, vb_buf, a_scr, b_scr, h_scr, vc_scr, y_scr, carry_scr,
                  *, alpha, ts, lw, k4, k31):
    t = pl.program_id(1)

    @pl.when(t == 0)
    def _():
        xa_buf[0:LRU_HALO, :] = jnp.zeros((LRU_HALO, lw), jnp.float32)
        vb_buf[0:CONV_HALO, :] = jnp.zeros((CONV_HALO, lw), jnp.float32)
        carry_scr[...] = jnp.zeros_like(carry_scr)

    sh = mod_ref[0:1, :]
    sc = mod_ref[1:2, :]
    gt = mod_ref[2:3, :]

    u_scr[...] = (x_ref[...] * (1.0 + sc) + sh).astype(jnp.bfloat16)
    u = u_scr[...]
    xa_buf[LRU_HALO:LRU_HALO + ts, :] = _dot(u, w_in_ref[:, 0:lw])
    ga_scr[...] = _dot(u, w_in_ref[:, lw:2 * lw])
    vb = _dot(u, w_in_ref[:, 2 * lw:3 * lw])
    gb = _dot(u, w_in_ref[:, 3 * lw:4 * lw])
    vb_buf[CONV_HALO:CONV_HALO + ts, :] = vb * _sigmoid(gb)

    xc = cb4_ref[...]
    for k in range(k4):
        off = LRU_HALO - (k4 - 1) + k
        xc = xc + cw4_ref[k:k + 1, :] * xa_buf[off:off + ts, :]
    neg_c_softplus = -LRU_C * jax.nn.softplus(-lam_ref[...])
    half = lw // 2
    for j in range(2):
        cols = slice(j * half, (j + 1) * half)
        xh = xc[:, cols]
        gates = _dot(xh.astype(jnp.bfloat16), wg_ref[j])
        r = _sigmoid(gates[:, 0:half] + br_ref[:, cols])
        i = _sigmoid(gates[:, half:2 * half] + bi_ref[:, cols])
        log_a = r * neg_c_softplus[:, cols]
        a = jnp.exp(log_a)
        mult = jnp.sqrt(-jnp.tanh(log_a) * (a * a + 1.0))
        a_scr[:, cols] = a
        b_scr[:, cols] = mult * (i * xh)

    row = lax.broadcasted_iota(jnp.int32, (SUBLANES, lw), 0)

    def scan_body(g, carry):
        r0 = pl.multiple_of(g * SUBLANES, SUBLANES)
        a = a_scr[pl.ds(r0, SUBLANES), :]
        b = b_scr[pl.ds(r0, SUBLANES), :]
        for s in (1, 2, 4):
            keep = row >= s
            a_prev = jnp.where(keep, pltpu.roll(a, s, axis=0), 1.0)
            b_prev = jnp.where(keep, pltpu.roll(b, s, axis=0), 0.0)
            b = a * b_prev + b
            a = a * a_prev
        h = b + a * carry
        h_scr[pl.ds(r0, SUBLANES), :] = h
        return h[SUBLANES - 1:SUBLANES, :]

    carry_scr[...] = lax.fori_loop(0, ts // SUBLANES, scan_body, carry_scr[...], unroll=2)
    y_scr[:, 0:lw] = (jax.nn.gelu(ga_scr[...]) * h_scr[...]).astype(jnp.bfloat16)

    rc = 16
    for c0 in range(0, ts, rc):
        acc = cb31_ref[...]
        for k in range(k31):
            off = CONV_HALO - (k31 - 1) + k + c0
            acc = acc + cw31_ref[k:k + 1, :] * vb_buf[off:off + rc, :]
        vc_scr[c0:c0 + rc, :] = acc

    def head_mean(v):
        hi = v.astype(jnp.bfloat16)
        lo = (v - hi.astype(jnp.float32)).astype(jnp.bfloat16)
        return _dot(hi, mavg_ref[...]) + _dot(lo, mavg_ref[...])

    for j in range(2):
        cols = slice(j * half, (j + 1) * half)
        vc = vc_scr[:, cols]
        d = vc - head_mean(vc)
        var = head_mean(d * d)
        yn = d * lax.rsqrt(var + LN_EPS) * ng_ref[:, cols] + nb_ref[:, cols]
        y_scr[:, lw + j * half:lw + (j + 1) * half] = (yn * _sigmoid(yn)).astype(jnp.bfloat16)

    y = _dot(y_scr[...], w_out_ref[...])
    z = alpha * x_ref[...] + (1.0 + gt) * y
    o_ref[...] = _layer_norm_rows(z, lng_ref[...], lnb_ref[...])

    xa_buf[0:LRU_HALO, :] = xa_buf[ts:ts + LRU_HALO, :]
    vb_buf[0:CONV_HALO, :] = vb_buf[ts:ts + CONV_HALO, :]


def _const_spec(shape):
    return pl.BlockSpec(shape, lambda b, t: (0,) * len(shape))


def _mixer(x, mod, w_in, cw4, cb4, wg, br, bi, lam, cw31, cb31, ng, nb, mavg, w_out, lng, lnb,
           *, alpha, ts):
    bsz, seq, d = x.shape
    lw = cw4.shape[1]
    k4 = cw4.shape[0]
    k31 = cw31.shape[0]
    assert seq % ts == 0 and k4 - 1 <= LRU_HALO and k31 - 1 <= CONV_HALO
    kern = functools.partial(_mixer_kernel, alpha=alpha, ts=ts, lw=lw, k4=k4, k31=k31)
    consts = (w_in, cw4, cb4, wg, br, bi, lam, cw31, cb31, ng, nb, mavg, w_out, lng, lnb)
    return pl.pallas_call(
        kern,
        grid=(bsz, seq // ts),
        in_specs=[
            pl.BlockSpec((None, ts, d), lambda b, t: (b, t, 0)),
            pl.BlockSpec((None, mod.shape[1], d), lambda b, t: (b, 0, 0)),
        ] + [_const_spec(a.shape) for a in consts],
        out_specs=pl.BlockSpec((None, ts, d), lambda b, t: (b, t, 0)),
        out_shape=jax.ShapeDtypeStruct(x.shape, jnp.float32),
        scratch_shapes=[
            pltpu.VMEM((ts, d), jnp.bfloat16),
            pltpu.VMEM((LRU_HALO + ts, lw), jnp.float32),
            pltpu.VMEM((ts, lw), jnp.float32),
            pltpu.VMEM((CONV_HALO + ts, lw), jnp.float32),
            pltpu.VMEM((ts, lw), jnp.float32),
            pltpu.VMEM((ts, lw), jnp.float32),
            pltpu.VMEM((ts, lw), jnp.float32),
            pltpu.VMEM((ts, lw), jnp.float32),
            pltpu.VMEM((ts, d), jnp.bfloat16),
            pltpu.VMEM((1, lw), jnp.float32),
        ],
        compiler_params=pltpu.CompilerParams(
            dimension_semantics=("arbitrary", "arbitrary"), vmem_limit_bytes=VMEM_LIMIT_BYTES),
        name="token_mixer",
    )(x, mod, *consts)


def _ffn_kernel(x_ref, mod_ref, w_up_ref, cw_ref, cb_ref, w_down_ref, lng_ref, lnb_ref,
                o_ref,
                u_scr, g_buf, act_scr,
                *, alpha, ts, dff, kc, fc):
    t = pl.program_id(1)

    @pl.when(t == 0)
    def _():
        g_buf[0:FFN_HALO, :] = jnp.zeros((FFN_HALO, dff), jnp.float32)

    sh = mod_ref[3:4, :]
    sc = mod_ref[4:5, :]
    gt = mod_ref[5:6, :]

    u_scr[...] = (x_ref[...] * (1.0 + sc) + sh).astype(jnp.bfloat16)
    u = u_scr[...]
    for c0 in range(0, dff, fc):
        cols = slice(c0, c0 + fc)
        g_buf[FFN_HALO:FFN_HALO + ts, cols] = _dot(u, w_up_ref[:, dff + c0:dff + c0 + fc])
        g = cb_ref[:, cols]
        for k in range(kc):
            off = FFN_HALO - (kc - 1) + k
            g = g + cw_ref[k:k + 1, cols] * g_buf[off:off + ts, cols]
        v = _dot(u, w_up_ref[:, cols])
        act_scr[:, cols] = (g * _sigmoid(g) * v).astype(jnp.bfloat16)

    y = _dot(act_scr[...], w_down_ref[...])
    z = alpha * x_ref[...] + (1.0 + gt) * y
    o_ref[...] = _layer_norm_rows(z, lng_ref[...], lnb_ref[...])

    g_buf[0:FFN_HALO, :] = g_buf[ts:ts + FFN_HALO, :]


def _ffn(x, mod, w_up, cw, cb, w_down, lng, lnb, *, alpha, ts):
    bsz, seq, d = x.shape
    kc, dff = cw.shape
    fc = 256
    assert seq % ts == 0 and dff % fc == 0 and kc - 1 <= FFN_HALO
    kern = functools.partial(_ffn_kernel, alpha=alpha, ts=ts, dff=dff, kc=kc, fc=fc)
    consts = (w_up, cw, cb, w_down, lng, lnb)
    return pl.pallas_call(
        kern,
        grid=(bsz, seq // ts),
        in_specs=[
            pl.BlockSpec((None, ts, d), lambda b, t: (b, t, 0)),
            pl.BlockSpec((None, mod.shape[1], d), lambda b, t: (b, 0, 0)),
        ] + [_const_spec(a.shape) for a in consts],
        out_specs=pl.BlockSpec((None, ts, d), lambda b, t: (b, t, 0)),
        out_shape=jax.ShapeDtypeStruct(x.shape, jnp.float32),
        scratch_shapes=[
            pltpu.VMEM((ts, d), jnp.bfloat16),
            pltpu.VMEM((FFN_HALO + ts, dff), jnp.float32),
            pltpu.VMEM((ts, dff), jnp.bfloat16),
        ],
        compiler_params=pltpu.CompilerParams(
            dimension_semantics=("arbitrary", "arbitrary"), vmem_limit_bytes=VMEM_LIMIT_BYTES),
        name="convglu_ffn",
    )(x, mod, *consts)


def _block_diag_tiles(w, tile):
    h, dh, _ = w.shape
    per = tile // dh
    w = w.reshape(h // per, per, dh, dh)
    eye = jnp.eye(per, dtype=w.dtype)
    return jnp.einsum("gpij,pq->gpiqj", w, eye).reshape(h // per, tile, tile)


def kernel(x, c, w_ada, b_ada, w_in, lru_conv_w, lru_conv_b, lru_w_r, lru_b_r, lru_w_i, lru_b_i,
           lru_lambda, conv_w, conv_b, conv_norm_g, conv_norm_b, w_out, ln1_g, ln1_b, ffn_w_up,
           ffn_conv_w, ffn_conv_b, ffn_w_down, ln2_g, ln2_b):
    depth = w_ada.shape[0]
    d = x.shape[-1]
    alpha = (2 * depth) ** 0.25
    bf16 = jnp.bfloat16
    lw = lru_conv_w.shape[-1]
    heads, head_dim = lru_w_r.shape[1], lru_w_r.shape[2]
    conv_heads = heads
    tile = lw // 2
    conv_head_dim = conv_w.shape[-1] // conv_heads
    mavg = jnp.kron(jnp.eye(tile // conv_head_dim, dtype=jnp.float32),
                    jnp.full((conv_head_dim, conv_head_dim), 1.0 / conv_head_dim, jnp.float32)).astype(bf16)
    row = lambda v: v.reshape(1, -1)

    for l in range(depth):
        mod = _modulation(c, w_ada[l], b_ada[l]).reshape(x.shape[0], 6, d)
        wg = jnp.concatenate([_block_diag_tiles(lru_w_r[l], tile), _block_diag_tiles(lru_w_i[l], tile)],
                             axis=-1).astype(bf16)
        x = _mixer(x, mod, w_in[l].astype(bf16), lru_conv_w[l], row(lru_conv_b[l]), wg,
                   row(lru_b_r[l]), row(lru_b_i[l]), row(lru_lambda[l]), conv_w[l], row(conv_b[l]),
                   row(conv_norm_g[l]), row(conv_norm_b[l]), mavg, w_out[l].astype(bf16),
                   row(ln1_g[l]), row(ln1_b[l]), alpha=alpha, ts=TIME_BLOCK)
        x = _ffn(x, mod, ffn_w_up[l].astype(bf16), ffn_conv_w[l], row(ffn_conv_b[l]),
                 ffn_w_down[l].astype(bf16), row(ln2_g[l]), row(ln2_b[l]), alpha=alpha, ts=TIME_BLOCK)
    return x
```

```python
import functools

import jax
import jax.numpy as jnp
from jax import lax
from jax.experimental import pallas as pl
from jax.experimental.pallas import tpu as pltpu

LRU_C = 8.0
LN_EPS = 1e-5
SUBLANES = 8
LRU_HALO = 8
CONV_HALO = 32
FFN_HALO = 8
VMEM_LIMIT_BYTES = 56 * 1024 * 1024
TIME_BLOCK = 256


def _dot(a, b):
    return jnp.dot(a, b, preferred_element_type=jnp.float32)


def _sigmoid(v):
    return 0.5 * jnp.tanh(0.5 * v) + 0.5


def _shift_rows(v, s):
    return pltpu.roll(v, s, axis=0)


def _layer_norm_rows(z, g, b):
    mu = jnp.mean(z, axis=-1, keepdims=True)
    d = z - mu
    var = jnp.mean(d * d, axis=-1, keepdims=True)
    return d * lax.rsqrt(var + LN_EPS) * g + b


def _mod_kernel(c_ref, w_ref, b_ref, o_ref):
    c = c_ref[...]
    c_act = c * _sigmoid(c)
    o_ref[...] = _dot(c_act, w_ref[...]) + b_ref[...]


def _modulation(c, w_ada, b_ada):
    bsz, d = c.shape
    n = w_ada.shape[1]
    return pl.pallas_call(
        _mod_kernel,
        grid=(n // d,),
        in_specs=[
            pl.BlockSpec((bsz, d), lambda j: (0, 0)),
            pl.BlockSpec((d, d), lambda j: (0, j)),
            pl.BlockSpec((1, d), lambda j: (0, j)),
        ],
        out_specs=pl.BlockSpec((bsz, d), lambda j: (0, j)),
        out_shape=jax.ShapeDtypeStruct((bsz, n), jnp.float32),
        compiler_params=pltpu.CompilerParams(
            dimension_semantics=("arbitrary",), vmem_limit_bytes=VMEM_LIMIT_BYTES),
        name="adaln_modulation",
    )(c, w_ada, b_ada.reshape(1, n))


def _mixer_kernel(x_ref, mod_ref, w_in_ref, cw4_ref, cb4_ref, wg_ref, br_ref, bi_ref, lam_ref,
                  cw31_ref, cb31_ref, ng_ref, nb_ref, mavg_ref, w_out_ref, lng_ref, lnb_ref,
                  o_ref,
                  u_scr, xa_buf, ga_scr, vb_buf, z_scr, a_scr, b_scr, h_scr, vc_scr, y_scr, carry_scr,
                  *, alpha, ts, lw, k4, k31):
    t = pl.program_id(1)

    @pl.when(t == 0)
    def _():
        xa_buf[0:LRU_HALO, :] = jnp.zeros((LRU_HALO, lw), jnp.float32)
        vb_buf[0:CONV_HALO, :] = jnp.zeros((CONV_HALO, lw), jnp.float32)
        carry_scr[...] = jnp.zeros_like(carry_scr)

    sh = mod_ref[0:1, :]
    sc = mod_ref[1:2, :]
    gt = mod_ref[2:3, :]

    u_scr[...] = (x_ref[...] * (1.0 + sc) + sh).astype(jnp.bfloat16)
    u = u_scr[...]
    xa_buf[LRU_HALO:LRU_HALO + ts, :] = _dot(u, w_in_ref[:, 0:lw])
    ga_scr[...] = _dot(u, w_in_ref[:, lw:2 * lw])
    vb = _dot(u, w_in_ref[:, 2 * lw:3 * lw])
    gb = _dot(u, w_in_ref[:, 3 * lw:4 * lw])
    vb_buf[CONV_HALO:CONV_HALO + ts, :] = vb * _sigmoid(gb)

    xa_ext = xa_buf[...]
    xc = cb4_ref[...] + cw4_ref[k4 - 1:k4, :] * xa_ext[LRU_HALO:, :]
    for s in range(1, k4):
        xc = xc + cw4_ref[k4 - 1 - s:k4 - s, :] * _shift_rows(xa_ext, s)[LRU_HALO:, :]
    neg_c_softplus = -LRU_C * jax.nn.softplus(-lam_ref[...])
    half = lw // 2
    for j in range(2):
        cols = slice(j * half, (j + 1) * half)
        xh = xc[:, cols]
        gates = _dot(xh.astype(jnp.bfloat16), wg_ref[j])
        r = _sigmoid(gates[:, 0:half] + br_ref[:, cols])
        i = _sigmoid(gates[:, half:2 * half] + bi_ref[:, cols])
        log_a = r * neg_c_softplus[:, cols]
        a = jnp.exp(log_a)
        mult = jnp.sqrt(-jnp.tanh(log_a) * (a * a + 1.0))
        a_scr[:, cols] = a
        b_scr[:, cols] = mult * (i * xh)

    row = lax.broadcasted_iota(jnp.int32, (SUBLANES, lw), 0)

    def scan_body(g, carry):
        r0 = pl.multiple_of(g * SUBLANES, SUBLANES)
        a = a_scr[pl.ds(r0, SUBLANES), :]
        b = b_scr[pl.ds(r0, SUBLANES), :]
        for s in (1, 2, 4):
            keep = row >= s
            a_prev = jnp.where(keep, pltpu.roll(a, s, axis=0), 1.0)
            b_prev = jnp.where(keep, pltpu.roll(b, s, axis=0), 0.0)
            b = a * b_prev + b
            a = a * a_prev
        h = b + a * carry
        h_scr[pl.ds(r0, SUBLANES), :] = h
        return h[SUBLANES - 1:SUBLANES, :]

    carry_scr[...] = lax.fori_loop(0, ts // SUBLANES, scan_body, carry_scr[...], unroll=2)
    y_scr[:, 0:lw] = (jax.nn.gelu(ga_scr[...]) * h_scr[...]).astype(jnp.bfloat16)

    vb_ext = vb_buf[...]
    for r in range(1, SUBLANES):
        z_scr[r - 1] = _shift_rows(vb_ext, r)[SUBLANES:, :]
    rc = 32
    groups = (rc // SUBLANES, SUBLANES, lw)
    for c0 in range(0, ts, rc):
        acc = jnp.broadcast_to(cb31_ref[...].reshape(1, 1, lw), groups)
        for s in range(k31):
            q, r = divmod(s, SUBLANES)
            w_tile = cw31_ref[k31 - 1 - s]
            if r == 0:
                off = CONV_HALO + c0 - SUBLANES * q
                src = vb_buf[off:off + rc, :]
            else:
                off = CONV_HALO - SUBLANES + c0 - SUBLANES * q
                src = z_scr[r - 1, off:off + rc, :]
            acc = acc + w_tile[None] * src.reshape(groups)
        vc_scr[c0:c0 + rc, :] = acc.reshape(rc, lw)

    def head_mean(v):
        hi = v.astype(jnp.bfloat16)
        lo = (v - hi.astype(jnp.float32)).astype(jnp.bfloat16)
        return _dot(hi, mavg_ref[...]) + _dot(lo, mavg_ref[...])

    for j in range(2):
        cols = slice(j * half, (j + 1) * half)
        vc = vc_scr[:, cols]
        d = vc - head_mean(vc)
        var = head_mean(d * d)
        yn = d * lax.rsqrt(var + LN_EPS) * ng_ref[:, cols] + nb_ref[:, cols]
        y_scr[:, lw + j * half:lw + (j + 1) * half] = (yn * _sigmoid(yn)).astype(jnp.bfloat16)

    y = _dot(y_scr[...], w_out_ref[...])
    z = alpha * x_ref[...] + (1.0 + gt) * y
    o_ref[...] = _layer_norm_rows(z, lng_ref[...], lnb_ref[...])

    xa_buf[0:LRU_HALO, :] = xa_buf[ts:ts + LRU_HALO, :]
    vb_buf[0:CONV_HALO, :] = vb_buf[ts:ts + CONV_HALO, :]


def _const_spec(shape):
    return pl.BlockSpec(shape, lambda b, t: (0,) * len(shape))


def _mixer(x, mod, w_in, cw4, cb4, wg, br, bi, lam, cw31, cb31, ng, nb, mavg, w_out, lng, lnb,
           *, alpha, ts):
    bsz, seq, d = x.shape
    lw = cw4.shape[1]
    k4 = cw4.shape[0]
    k31 = cw31.shape[0]
    assert seq % ts == 0 and k4 - 1 <= LRU_HALO and k31 - 1 <= CONV_HALO
    kern = functools.partial(_mixer_kernel, alpha=alpha, ts=ts, lw=lw, k4=k4, k31=k31)
    cw31 = jnp.broadcast_to(cw31[:, None, :], (k31, SUBLANES, lw))
    consts = (w_in, cw4, cb4, wg, br, bi, lam, cw31, cb31, ng, nb, mavg, w_out, lng, lnb)
    return pl.pallas_call(
        kern,
        grid=(bsz, seq // ts),
        in_specs=[
            pl.BlockSpec((None, ts, d), lambda b, t: (b, t, 0)),
            pl.BlockSpec((None, mod.shape[1], d), lambda b, t: (b, 0, 0)),
        ] + [_const_spec(a.shape) for a in consts],
        out_specs=pl.BlockSpec((None, ts, d), lambda b, t: (b, t, 0)),
        out_shape=jax.ShapeDtypeStruct(x.shape, jnp.float32),
        scratch_shapes=[
            pltpu.VMEM((ts, d), jnp.bfloat16),
            pltpu.VMEM((LRU_HALO + ts, lw), jnp.float32),
            pltpu.VMEM((ts, lw), jnp.float32),
            pltpu.VMEM((CONV_HALO + ts, lw), jnp.float32),
            pltpu.VMEM((SUBLANES - 1, CONV_HALO - SUBLANES + ts, lw), jnp.float32),
            pltpu.VMEM((ts, lw), jnp.float32),
            pltpu.VMEM((ts, lw), jnp.float32),
            pltpu.VMEM((ts, lw), jnp.float32),
            pltpu.VMEM((ts, lw), jnp.float32),
            pltpu.VMEM((ts, d), jnp.bfloat16),
            pltpu.VMEM((1, lw), jnp.float32),
        ],
        compiler_params=pltpu.CompilerParams(
            dimension_semantics=("arbitrary", "arbitrary"), vmem_limit_bytes=VMEM_LIMIT_BYTES),
        name="token_mixer",
    )(x, mod, *consts)


def _ffn_kernel(x_ref, mod_ref, w_up_ref, cw_ref, cb_ref, w_down_ref, lng_ref, lnb_ref,
                o_ref,
                u_scr, g_buf, act_scr,
                *, alpha, ts, dff, kc, fc):
    t = pl.program_id(1)

    @pl.when(t == 0)
    def _():
        g_buf[0:FFN_HALO, :] = jnp.zeros((FFN_HALO, dff), jnp.float32)

    sh = mod_ref[3:4, :]
    sc = mod_ref[4:5, :]
    gt = mod_ref[5:6, :]

    u_scr[...] = (x_ref[...] * (1.0 + sc) + sh).astype(jnp.bfloat16)
    u = u_scr[...]
    for c0 in range(0, dff, fc):
        cols = slice(c0, c0 + fc)
        g_buf[FFN_HALO:FFN_HALO + ts, cols] = _dot(u, w_up_ref[:, dff + c0:dff + c0 + fc])
        g_ext = g_buf[:, cols]
        g = cb_ref[:, cols] + cw_ref[kc - 1:kc, cols] * g_ext[FFN_HALO:, :]
        for s in range(1, kc):
            g = g + cw_ref[kc - 1 - s:kc - s, cols] * _shift_rows(g_ext, s)[FFN_HALO:, :]
        v = _dot(u, w_up_ref[:, cols])
        act_scr[:, cols] = (g * _sigmoid(g) * v).astype(jnp.bfloat16)

    y = _dot(act_scr[...], w_down_ref[...])
    z = alpha * x_ref[...] + (1.0 + gt) * y
    o_ref[...] = _layer_norm_rows(z, lng_ref[...], lnb_ref[...])

    g_buf[0:FFN_HALO, :] = g_buf[ts:ts + FFN_HALO, :]


def _ffn(x, mod, w_up, cw, cb, w_down, lng, lnb, *, alpha, ts):
    bsz, seq, d = x.shape
    kc, dff = cw.shape
    fc = 256
    assert seq % ts == 0 and dff % fc == 0 and kc - 1 <= FFN_HALO
    kern = functools.partial(_ffn_kernel, alpha=alpha, ts=ts, dff=dff, kc=kc, fc=fc)
    consts = (w_up, cw, cb, w_down, lng, lnb)
    return pl.pallas_call(
        kern,
        grid=(bsz, seq // ts),
        in_specs=[
            pl.BlockSpec((None, ts, d), lambda b, t: (b, t, 0)),
            pl.BlockSpec((None, mod.shape[1], d), lambda b, t: (b, 0, 0)),
        ] + [_const_spec(a.shape) for a in consts],
        out_specs=pl.BlockSpec((None, ts, d), lambda b, t: (b, t, 0)),
        out_shape=jax.ShapeDtypeStruct(x.shape, jnp.float32),
        scratch_shapes=[
            pltpu.VMEM((ts, d), jnp.bfloat16),
            pltpu.VMEM((FFN_HALO + ts, dff), jnp.float32),
            pltpu.VMEM((ts, dff), jnp.bfloat16),
        ],
        compiler_params=pltpu.CompilerParams(
            dimension_semantics=("arbitrary", "arbitrary"), vmem_limit_bytes=VMEM_LIMIT_BYTES),
        name="convglu_ffn",
    )(x, mod, *consts)


def _block_diag_tiles(w, tile):
    h, dh, _ = w.shape
    per = tile // dh
    w = w.reshape(h // per, per, dh, dh)
    eye = jnp.eye(per, dtype=w.dtype)
    return jnp.einsum("gpij,pq->gpiqj", w, eye).reshape(h // per, tile, tile)


def kernel(x, c, w_ada, b_ada, w_in, lru_conv_w, lru_conv_b, lru_w_r, lru_b_r, lru_w_i, lru_b_i,
           lru_lambda, conv_w, conv_b, conv_norm_g, conv_norm_b, w_out, ln1_g, ln1_b, ffn_w_up,
           ffn_conv_w, ffn_conv_b, ffn_w_down, ln2_g, ln2_b):
    depth = w_ada.shape[0]
    d = x.shape[-1]
    alpha = (2 * depth) ** 0.25
    bf16 = jnp.bfloat16
    lw = lru_conv_w.shape[-1]
    heads, head_dim = lru_w_r.shape[1], lru_w_r.shape[2]
    conv_heads = heads
    tile = lw // 2
    conv_head_dim = conv_w.shape[-1] // conv_heads
    mavg = jnp.kron(jnp.eye(tile // conv_head_dim, dtype=jnp.float32),
                    jnp.full((conv_head_dim, conv_head_dim), 1.0 / conv_head_dim, jnp.float32)).astype(bf16)
    row = lambda v: v.reshape(1, -1)

    for l in range(depth):
        mod = _modulation(c, w_ada[l], b_ada[l]).reshape(x.shape[0], 6, d)
        wg = jnp.concatenate([_block_diag_tiles(lru_w_r[l], tile), _block_diag_tiles(lru_w_i[l], tile)],
                             axis=-1).astype(bf16)
        x = _mixer(x, mod, w_in[l].astype(bf16), lru_conv_w[l], row(lru_conv_b[l]), wg,
                   row(lru_b_r[l]), row(lru_b_i[l]), row(lru_lambda[l]), conv_w[l], row(conv_b[l]),
                   row(conv_norm_g[l]), row(conv_norm_b[l]), mavg, w_out[l].astype(bf16),
                   row(ln1_g[l]), row(ln1_b[l]), alpha=alpha, ts=TIME_BLOCK)
        x = _ffn(x, mod, ffn_w_up[l].astype(bf16), ffn_conv_w[l], row(ffn_conv_b[l]),
                 ffn_w_down[l].astype(bf16), row(ln2_g[l]), row(ln2_b[l]), alpha=alpha, ts=TIME_BLOCK)
    return x
```

```python
import functools

import jax
import jax.numpy as jnp
from jax import lax
from jax.experimental import pallas as pl
from jax.experimental.pallas import tpu as pltpu

LRU_C = 8.0
LN_EPS = 1e-5
SUBLANES = 8
LRU_HALO = 8
CONV_HALO = 32
FFN_HALO = 8
FFN_COLS = 256
CONV_ROWS = 32
VMEM_LIMIT_BYTES = 58 * 1024 * 1024
TIME_BLOCK = 256
GELU_K = 0.7978845608028654
GELU_C = 0.044715


def _dot(a, b):
    return jnp.dot(a, b, preferred_element_type=jnp.float32)


def _sigmoid(v):
    return 0.5 * jnp.tanh(0.5 * v) + 0.5


def _silu_from_half(h):
    return h + h * jnp.tanh(h)


def _shift_rows(v, s):
    return pltpu.roll(v, s, axis=0)


def _layer_norm_rows(z, g, b):
    mu = jnp.mean(z, axis=-1, keepdims=True)
    d = z - mu
    var = jnp.mean(d * d, axis=-1, keepdims=True)
    return d * lax.rsqrt(var + LN_EPS) * g + b


def _mod_kernel(c_ref, w_ref, b_ref, o_ref):
    c = c_ref[...]
    c_act = c * _sigmoid(c)
    o_ref[...] = _dot(c_act, w_ref[...]) + b_ref[...]


def _modulation(c, w_ada, b_ada):
    bsz, d = c.shape
    n = w_ada.shape[1]
    return pl.pallas_call(
        _mod_kernel,
        grid=(n // d,),
        in_specs=[
            pl.BlockSpec((bsz, d), lambda j: (0, 0)),
            pl.BlockSpec((d, d), lambda j: (0, j)),
            pl.BlockSpec((1, d), lambda j: (0, j)),
        ],
        out_specs=pl.BlockSpec((bsz, d), lambda j: (0, j)),
        out_shape=jax.ShapeDtypeStruct((bsz, n), jnp.float32),
        compiler_params=pltpu.CompilerParams(
            dimension_semantics=("arbitrary",), vmem_limit_bytes=VMEM_LIMIT_BYTES),
        name="adaln_modulation",
    )(c, w_ada, b_ada.reshape(1, n))


def _block_kernel(x_ref, mod_ref, modp_ref,
                  w_in_ref, cw4_ref, cb4_ref, wg_ref, br_ref, bi_ref, lam_ref,
                  cw31_ref, cb31_ref, ng_ref, nb_ref, mavg_ref, w_out_ref, ln1g_ref, ln1b_ref,
                  w_up_ref, cwf_ref, cbf_ref, w_down_ref, ln2g_ref, ln2b_ref,
                  o_ref,
                  u_scr, xa_buf, ga_scr, vb_buf, z_scr, a_scr, b_scr, h_scr,
                  vc_scr, y_scr, carry_scr, x1_scr, u2_scr, g_buf, act_scr, y1_scr, y2_scr,
                  *, alpha, ts, nt, lw, k4, k31, dff, kc):
    i = pl.program_id(0)
    t_mix = lax.rem(i, nt)
    t_ffn = lax.rem(i + nt - 1, nt)
    half = lw // 2
    bf16 = jnp.bfloat16

    @pl.when(i == 0)
    def _():
        x1_scr[...] = jnp.zeros_like(x1_scr)

    @pl.when(t_mix == 0)
    def _():
        xa_buf[0:LRU_HALO, :] = jnp.zeros((LRU_HALO, lw), jnp.float32)
        vb_buf[0:CONV_HALO, :] = jnp.zeros((CONV_HALO, lw), jnp.float32)
        carry_scr[...] = jnp.zeros_like(carry_scr)

    @pl.when(jnp.logical_or(t_ffn == 0, i == 0))
    def _():
        g_buf[0:FFN_HALO, :] = jnp.zeros((FFN_HALO, dff), jnp.float32)


    def in_mod():
        sh, sc = mod_ref[0:1, :], mod_ref[1:2, :]
        u_scr[...] = (x_ref[...] * (1.0 + sc) + sh).astype(bf16)

    def in_proj_xa():
        xa_buf[LRU_HALO:LRU_HALO + ts, :] = _dot(u_scr[...], w_in_ref[:, 0:lw])

    def in_proj_ga():
        ga_scr[...] = _dot(u_scr[...], w_in_ref[:, lw:2 * lw])

    def in_proj_glu():
        u = u_scr[...]
        vb = _dot(u, w_in_ref[:, 2 * lw:3 * lw])
        gb = _dot(u, w_in_ref[:, 3 * lw:4 * lw])
        vb_buf[CONV_HALO:CONV_HALO + ts, :] = vb * _sigmoid(gb)

    def move_halos():
        xa_buf[0:LRU_HALO, :] = xa_buf[ts:ts + LRU_HALO, :]
        vb_buf[0:CONV_HALO, :] = vb_buf[ts:ts + CONV_HALO, :]


    def mix_gates(j):
        cols = slice(j * half, (j + 1) * half)
        xa_ext = xa_buf[:, cols]
        xh = cb4_ref[:, cols] + cw4_ref[k4 - 1:k4, cols] * xa_ext[LRU_HALO:, :]
        for s in range(1, k4):
            xh = xh + cw4_ref[k4 - 1 - s:k4 - s, cols] * _shift_rows(xa_ext, s)[LRU_HALO:, :]
        gates = _dot(xh.astype(bf16), wg_ref[j])
        r = _sigmoid(gates[:, 0:half] + br_ref[:, cols])
        gi = _sigmoid(gates[:, half:2 * half] + bi_ref[:, cols])
        log_a = r * (-LRU_C * jax.nn.softplus(-lam_ref[:, cols]))
        a = jnp.exp(log_a)
        mult = jnp.sqrt(-jnp.tanh(log_a) * (a * a + 1.0))
        a_scr[:, cols] = a
        b_scr[:, cols] = mult * (gi * xh)

    row = lax.broadcasted_iota(jnp.int32, (SUBLANES, lw), 0)
    keeps = {s: row >= s for s in (1, 2, 4)}

    def mix_scan():
        carry = carry_scr[...]
        for g in range(ts // SUBLANES):
            r0 = g * SUBLANES
            a = a_scr[r0:r0 + SUBLANES, :]
            b = b_scr[r0:r0 + SUBLANES, :]
            for s in (1, 2, 4):
                a_prev = jnp.where(keeps[s], pltpu.roll(a, s, axis=0), 1.0)
                b_prev = jnp.where(keeps[s], pltpu.roll(b, s, axis=0), 0.0)
                b = a * b_prev + b
                a = a * a_prev
            h = b + a * carry
            h_scr[r0:r0 + SUBLANES, :] = h
            carry = h[SUBLANES - 1:SUBLANES, :]
        carry_scr[...] = carry

    def mix_gelu():
        ga = ga_scr[...]
        inner = ga * (ga * ga * (GELU_K * GELU_C) + GELU_K)
        hx = 0.5 * ga
        y_scr[:, 0:lw] = ((hx + hx * jnp.tanh(inner)) * h_scr[...]).astype(bf16)

    def mix_shift_copy(r):
        z_scr[r - 1] = _shift_rows(vb_buf[...], r)[SUBLANES:, :]

    groups = (CONV_ROWS // SUBLANES, SUBLANES, lw)

    def mix_conv(c0):
        acc = jnp.broadcast_to(cb31_ref[...].reshape(1, 1, lw), groups)
        for s in range(k31):
            q, r = divmod(s, SUBLANES)
            w_tile = cw31_ref[k31 - 1 - s]
            if r == 0:
                off = CONV_HALO + c0 - SUBLANES * q
                src = vb_buf[off:off + CONV_ROWS, :]
            else:
                off = CONV_HALO - SUBLANES + c0 - SUBLANES * q
                src = z_scr[r - 1, off:off + CONV_ROWS, :]
            acc = acc + w_tile[None] * src.reshape(groups)
        vc_scr[c0:c0 + CONV_ROWS, :] = acc.reshape(CONV_ROWS, lw)

    def mix_head_norm(j):
        cols = slice(j * half, (j + 1) * half)
        vc = vc_scr[:, cols]
        d = vc - _dot(vc.astype(bf16), mavg_ref[...])
        var = _dot((d * d).astype(bf16), mavg_ref[...])
        hn = d * lax.rsqrt(var + LN_EPS) * (0.5 * ng_ref[:, cols]) + 0.5 * nb_ref[:, cols]
        y_scr[:, lw + j * half:lw + (j + 1) * half] = _silu_from_half(hn).astype(bf16)

    def mix_out_lru():
        y1_scr[...] = _dot(y_scr[:, 0:lw], w_out_ref[0:lw, :])

    def mix_out():
        gt = mod_ref[2:3, :]
        y = y1_scr[...] + _dot(y_scr[:, lw:], w_out_ref[lw:, :])
        z = alpha * x_ref[...] + (1.0 + gt) * y
        x1_scr[...] = _layer_norm_rows(z, ln1g_ref[...], ln1b_ref[...])


    def ffn_in():
        sh, sc = modp_ref[3:4, :], modp_ref[4:5, :]
        u2_scr[...] = (x1_scr[...] * (1.0 + sc) + sh).astype(bf16)

    def ffn_chunk(c0):
        cols = slice(c0, c0 + FFN_COLS)
        u2 = u2_scr[...]
        g_buf[FFN_HALO:FFN_HALO + ts, cols] = _dot(u2, w_up_ref[:, dff + c0:dff + c0 + FFN_COLS])
        g_ext = g_buf[:, cols]
        hg = 0.5 * cbf_ref[:, cols] + (0.5 * cwf_ref[kc - 1:kc, cols]) * g_ext[FFN_HALO:, :]
        for s in range(1, kc):
            hg = hg + (0.5 * cwf_ref[kc - 1 - s:kc - s, cols]) * _shift_rows(g_ext, s)[FFN_HALO:, :]
        v = _dot(u2, w_up_ref[:, cols])
        act_scr[:, cols] = (_silu_from_half(hg) * v).astype(bf16)

    def ffn_down(k0, k1):
        part = _dot(act_scr[:, k0:k1], w_down_ref[k0:k1, :])
        if k0 == 0:
            y2_scr[...] = part
        else:
            y2_scr[...] += part

    def ffn_out(k0):
        gt = modp_ref[5:6, :]
        y = y2_scr[...] + _dot(act_scr[:, k0:], w_down_ref[k0:, :])
        z = alpha * x1_scr[...] + (1.0 + gt) * y
        o_ref[...] = _layer_norm_rows(z, ln2g_ref[...], ln2b_ref[...])
        g_buf[0:FFN_HALO, :] = g_buf[ts:ts + FFN_HALO, :]

    def region(*pieces):
        for piece in pieces:
            piece()

    chunk = lambda n: functools.partial(ffn_chunk, n * FFN_COLS)
    down = lambda n0, n1: functools.partial(ffn_down, n0 * FFN_COLS, n1 * FFN_COLS)
    copy = lambda r: functools.partial(mix_shift_copy, r)
    conv = lambda n: functools.partial(mix_conv, n * CONV_ROWS)
    gates = lambda j: functools.partial(mix_gates, j)
    head = lambda j: functools.partial(mix_head_norm, j)
    assert dff == 11 * FFN_COLS and ts == 8 * CONV_ROWS

    region(ffn_in, in_mod, in_proj_xa, in_proj_ga, in_proj_glu)
    region(chunk(0), gates(0), gates(1))
    region(chunk(1), mix_scan)
    region(chunk(2), mix_gelu, copy(1))
    region(chunk(3), down(0, 3), copy(2), copy(3))
    region(chunk(4), mix_out_lru, copy(4), copy(5))
    region(chunk(5), copy(6), copy(7))
    region(chunk(6), down(3, 6), conv(0), conv(1))
    region(chunk(7), conv(2), conv(3))
    region(chunk(8), conv(4), conv(5))
    region(chunk(9), down(6, 9), conv(6), conv(7))
    region(chunk(10), head(0), head(1))
    region(functools.partial(ffn_out, 9 * FFN_COLS), mix_out, move_halos)


def _block(x, mod, mixer_consts, ffn_consts, *, alpha, ts):
    bsz, seq, d = x.shape
    (w_in, cw4, cb4, wg, br, bi, lam, cw31, cb31, ng, nb, mavg, w_out, ln1g, ln1b) = mixer_consts
    (w_up, cwf, cbf, w_down, ln2g, ln2b) = ffn_consts
    k4, lw = cw4.shape
    k31 = cw31.shape[0]
    kc, dff = cwf.shape
    nt = seq // ts
    n_blocks = bsz * nt
    assert seq % ts == 0 and ts % CONV_ROWS == 0 and dff % FFN_COLS == 0
    assert k4 - 1 <= LRU_HALO and k31 - 1 <= CONV_HALO and kc - 1 <= FFN_HALO
    cw31 = jnp.broadcast_to(cw31[:, None, :], (k31, SUBLANES, lw))
    consts = (w_in, cw4, cb4, wg, br, bi, lam, cw31, cb31, ng, nb, mavg, w_out, ln1g, ln1b,
              w_up, cwf, cbf, w_down, ln2g, ln2b)
    kern = functools.partial(_block_kernel, alpha=alpha, ts=ts, nt=nt, lw=lw, k4=k4, k31=k31,
                             dff=dff, kc=kc)

    def lagged(lag):
        return lambda i: jnp.clip(i - lag, 0, n_blocks - 1)

    def x_spec(lag):
        blk = lagged(lag)
        return pl.BlockSpec((None, ts, d), lambda i: (blk(i) // nt, blk(i) % nt, 0))

    def mod_spec(lag):
        blk = lagged(lag)
        return pl.BlockSpec((None, mod.shape[1], d), lambda i: (blk(i) // nt, 0, 0))

    def const_spec(a):
        return pl.BlockSpec(a.shape, lambda i: (0,) * a.ndim, pipeline_mode=pl.Buffered(1))

    return pl.pallas_call(
        kern,
        grid=(n_blocks + 1,),
        in_specs=[x_spec(0), mod_spec(0), mod_spec(1)] + [const_spec(a) for a in consts],
        out_specs=x_spec(1),
        out_shape=jax.ShapeDtypeStruct(x.shape, jnp.float32),
        scratch_shapes=[
            pltpu.VMEM((ts, d), jnp.bfloat16),
            pltpu.VMEM((LRU_HALO + ts, lw), jnp.float32),
            pltpu.VMEM((ts, lw), jnp.float32),
            pltpu.VMEM((CONV_HALO + ts, lw), jnp.float32),
            pltpu.VMEM((SUBLANES - 1, CONV_HALO - SUBLANES + ts, lw), jnp.float32),
            pltpu.VMEM((ts, lw), jnp.float32),
            pltpu.VMEM((ts, lw), jnp.float32),
            pltpu.VMEM((ts, lw), jnp.float32),
            pltpu.VMEM((ts, lw), jnp.float32),
            pltpu.VMEM((ts, d), jnp.bfloat16),
            pltpu.VMEM((1, lw), jnp.float32),
            pltpu.VMEM((ts, d), jnp.float32),
            pltpu.VMEM((ts, d), jnp.bfloat16),
            pltpu.VMEM((FFN_HALO + ts, dff), jnp.float32),
            pltpu.VMEM((ts, dff), jnp.bfloat16),
            pltpu.VMEM((ts, d), jnp.float32),
            pltpu.VMEM((ts, d), jnp.float32),
        ],
        compiler_params=pltpu.CompilerParams(
            dimension_semantics=("arbitrary",), vmem_limit_bytes=VMEM_LIMIT_BYTES),
        name="mixer_ffn_block",
    )(x, mod, mod, *consts)


def _block_diag_tiles(w, tile):
    h, dh, _ = w.shape
    per = tile // dh
    w = w.reshape(h // per, per, dh, dh)
    eye = jnp.eye(per, dtype=w.dtype)
    return jnp.einsum("gpij,pq->gpiqj", w, eye).reshape(h // per, tile, tile)


def kernel(x, c, w_ada, b_ada, w_in, lru_conv_w, lru_conv_b, lru_w_r, lru_b_r, lru_w_i, lru_b_i,
           lru_lambda, conv_w, conv_b, conv_norm_g, conv_norm_b, w_out, ln1_g, ln1_b, ffn_w_up,
           ffn_conv_w, ffn_conv_b, ffn_w_down, ln2_g, ln2_b):
    depth = w_ada.shape[0]
    d = x.shape[-1]
    alpha = (2 * depth) ** 0.25
    bf16 = jnp.bfloat16
    lw = lru_conv_w.shape[-1]
    conv_heads = lru_w_r.shape[1]
    tile = lw // 2
    conv_head_dim = conv_w.shape[-1] // conv_heads
    mavg = jnp.kron(jnp.eye(tile // conv_head_dim, dtype=jnp.float32),
                    jnp.full((conv_head_dim, conv_head_dim), 1.0 / conv_head_dim, jnp.float32)).astype(bf16)
    row = lambda v: v.reshape(1, -1)

    for l in range(depth):
        mod = _modulation(c, w_ada[l], b_ada[l]).reshape(x.shape[0], 6, d)
        wg = jnp.concatenate([_block_diag_tiles(lru_w_r[l], tile), _block_diag_tiles(lru_w_i[l], tile)],
                             axis=-1).astype(bf16)
        mixer_consts = (w_in[l].astype(bf16), lru_conv_w[l], row(lru_conv_b[l]), wg, row(lru_b_r[l]),
                        row(lru_b_i[l]), row(lru_lambda[l]), conv_w[l], row(conv_b[l]),
                        row(conv_norm_g[l]), row(conv_norm_b[l]), mavg, w_out[l].astype(bf16),
                        row(ln1_g[l]), row(ln1_b[l]))
        ffn_consts = (ffn_w_up[l].astype(bf16), ffn_conv_w[l], row(ffn_conv_b[l]),
                      ffn_w_down[l].astype(bf16), row(ln2_g[l]), row(ln2_b[l]))
        x = _block(x, mod, mixer_consts, ffn_consts, alpha=alpha, ts=TIME_BLOCK)
    return x
```

```python
import functools

import jax
import jax.numpy as jnp
from jax import lax
from jax.experimental import pallas as pl
from jax.experimental.pallas import tpu as pltpu

LRU_C = 8.0
LN_EPS = 1e-5
SUBLANES = 8
LANES = 128
TIME_STEPS = 32
FFN_COLS = 256
CONV_STEPS = 4
VMEM_LIMIT_BYTES = 58 * 1024 * 1024
GELU_K = 0.7978845608028654
GELU_C = 0.044715


def _dot(a, b):
    return jnp.dot(a, b, preferred_element_type=jnp.float32)


def _sigmoid(v):
    return 0.5 * jnp.tanh(0.5 * v) + 0.5


def _silu_from_half(h):
    return h + h * jnp.tanh(h)


def _layer_norm_rows(z, g, b):
    mu = jnp.mean(z, axis=-1, keepdims=True)
    d = z - mu
    var = jnp.mean(d * d, axis=-1, keepdims=True)
    return d * lax.rsqrt(var + LN_EPS) * g + b


def _scale_shift(v, scale_tile, shift_tile=None):
    v3 = v.reshape(v.shape[0] // SUBLANES, SUBLANES, v.shape[1]) * scale_tile[None]
    if shift_tile is not None:
        v3 = v3 + shift_tile[None]
    return v3.reshape(v.shape)


def _mod_kernel(c_ref, w_ref, b_ref, o_ref):
    c = c_ref[...]
    c_act = c * _sigmoid(c)
    o_ref[...] = _dot(c_act, w_ref[...]) + b_ref[...]


def _modulation(c, w_ada, b_ada):
    bsz, d = c.shape
    n = w_ada.shape[1]
    return pl.pallas_call(
        _mod_kernel,
        grid=(n // d,),
        in_specs=[
            pl.BlockSpec((bsz, d), lambda j: (0, 0)),
            pl.BlockSpec((d, d), lambda j: (0, j)),
            pl.BlockSpec((1, d), lambda j: (0, j)),
        ],
        out_specs=pl.BlockSpec((bsz, d), lambda j: (0, j)),
        out_shape=jax.ShapeDtypeStruct((bsz, n), jnp.float32),
        compiler_params=pltpu.CompilerParams(
            dimension_semantics=("arbitrary",), vmem_limit_bytes=VMEM_LIMIT_BYTES),
        name="adaln_modulation",
    )(c, w_ada, b_ada.reshape(1, n))


def _block_kernel(x_ref, mod_ref, modp_ref,
                  w_in_ref, cw4_ref, cb4_ref, wg_ref, br_ref, bi_ref, lam_ref,
                  cw31_ref, cb31_ref, ng_ref, nb_ref, mavg_ref, w_out_ref, ln1g_ref, ln1b_ref,
                  w_up_ref, cwf_ref, cbf_ref, w_down_ref, ln2g_ref, ln2b_ref,
                  o_ref,
                  x_tb, u_scr, xa_buf, ga_scr, vb_buf, a_scr, b_scr, h_scr, vc_scr, y_scr,
                  carry_scr, x1_scr, u2_scr, g_buf, act_scr, y1_scr, y2_scr, res_tb,
                  res2_scr,
                  *, alpha, tt, nt, lw, k4, k31, dff, kc):
    i = pl.program_id(0)
    t_mix = lax.rem(i, nt)
    t_ffn = lax.rem(i + nt - 1, nt)
    rows = tt * SUBLANES
    d = x1_scr.shape[1]
    half = lw // 2
    bf16 = jnp.bfloat16
    halo4 = (k4 - 1) * SUBLANES
    halo31 = (k31 - 1) * SUBLANES
    halof = (kc - 1) * SUBLANES

    @pl.when(i == 0)
    def _():
        x1_scr[...] = jnp.zeros_like(x1_scr)

    @pl.when(t_mix == 0)
    def _():
        xa_buf[0:halo4, :] = jnp.zeros((halo4, lw), jnp.float32)
        vb_buf[0:halo31, :] = jnp.zeros((halo31, lw), jnp.float32)
        carry_scr[...] = jnp.zeros_like(carry_scr)

    @pl.when(jnp.logical_or(t_ffn == 0, i == 0))
    def _():
        g_buf[0:halof, :] = jnp.zeros((halof, dff), jnp.float32)

    def time_major(ref3):
        return jnp.concatenate([ref3[c] for c in range(d // LANES)], axis=1)


    def to_time_major():
        for b in range(SUBLANES):
            for c in range(d // LANES):
                x_tb[c, pl.ds(b, tt, stride=SUBLANES), :] = x_ref[b, :, c * LANES:(c + 1) * LANES]

    def in_mod():
        u_scr[...] = _scale_shift(time_major(x_tb), 1.0 + mod_ref[1], mod_ref[0]).astype(bf16)

    def in_proj_xa():
        xa_buf[halo4:halo4 + rows, :] = _dot(u_scr[...], w_in_ref[:, 0:lw])

    def in_proj_ga():
        ga_scr[...] = _dot(u_scr[...], w_in_ref[:, lw:2 * lw])

    def in_proj_glu():
        u = u_scr[...]
        vb = _dot(u, w_in_ref[:, 2 * lw:3 * lw])
        gb = _dot(u, w_in_ref[:, 3 * lw:4 * lw])
        vb_buf[halo31:halo31 + rows, :] = vb * _sigmoid(gb)

    def move_history():
        xa_buf[0:halo4, :] = xa_buf[rows:rows + halo4, :]
        vb_buf[0:halo31, :] = vb_buf[rows:rows + halo31, :]

    def mix_gates(j):
        cols = slice(j * half, (j + 1) * half)
        xh = cb4_ref[:, cols]
        for s in range(k4):
            off = halo4 - s * SUBLANES
            xh = xh + cw4_ref[k4 - 1 - s:k4 - s, cols] * xa_buf[off:off + rows, cols]
        gates = _dot(xh.astype(bf16), wg_ref[j])
        r = _sigmoid(gates[:, 0:half] + br_ref[:, cols])
        gi = _sigmoid(gates[:, half:2 * half] + bi_ref[:, cols])
        log_a = r * (-LRU_C * jax.nn.softplus(-lam_ref[:, cols]))
        a = jnp.exp(log_a)
        mult = jnp.sqrt(-jnp.tanh(log_a) * (a * a + 1.0))
        a_scr[:, cols] = a
        b_scr[:, cols] = mult * (gi * xh)

    def mix_scan():
        h = carry_scr[...]
        for t in range(tt):
            r0 = t * SUBLANES
            h = a_scr[r0:r0 + SUBLANES, :] * h + b_scr[r0:r0 + SUBLANES, :]
            h_scr[r0:r0 + SUBLANES, :] = h
        carry_scr[...] = h

    def mix_gelu():
        ga = ga_scr[...]
        inner = ga * (ga * ga * (GELU_K * GELU_C) + GELU_K)
        hx = 0.5 * ga
        y_scr[:, 0:lw] = ((hx + hx * jnp.tanh(inner)) * h_scr[...]).astype(bf16)

    conv_rows = CONV_STEPS * SUBLANES
    groups = (CONV_STEPS, SUBLANES, lw)

    def mix_conv(c0):
        acc = jnp.broadcast_to(cb31_ref[...].reshape(1, 1, lw), groups)
        for s in range(k31):
            w_tile = cw31_ref[k31 - 1 - s]
            off = halo31 - s * SUBLANES + c0
            acc = acc + w_tile[None] * vb_buf[off:off + conv_rows, :].reshape(groups)
        vc_scr[c0:c0 + conv_rows, :] = acc.reshape(conv_rows, lw)

    def mix_head_norm(j):
        cols = slice(j * half, (j + 1) * half)
        vc = vc_scr[:, cols]
        dv = vc - _dot(vc.astype(bf16), mavg_ref[...])
        var = _dot((dv * dv).astype(bf16), mavg_ref[...])
        hn = dv * lax.rsqrt(var + LN_EPS) * (0.5 * ng_ref[:, cols]) + 0.5 * nb_ref[:, cols]
        y_scr[:, lw + j * half:lw + (j + 1) * half] = _silu_from_half(hn).astype(bf16)

    def mix_out_lru():
        y1_scr[...] = _dot(y_scr[:, 0:lw], w_out_ref[0:lw, :])

    def mix_out():
        y = y1_scr[...] + _dot(y_scr[:, lw:], w_out_ref[lw:, :])
        z = alpha * time_major(x_tb) + _scale_shift(y, 1.0 + mod_ref[2])
        x1_scr[...] = _layer_norm_rows(z, ln1g_ref[...], ln1b_ref[...])


    def ffn_in():
        x1 = x1_scr[...]
        u2_scr[...] = _scale_shift(x1, 1.0 + modp_ref[4], modp_ref[3]).astype(bf16)
        res2_scr[...] = alpha * x1

    def ffn_chunk(c0):
        cols = slice(c0, c0 + FFN_COLS)
        u2 = u2_scr[...]
        g_buf[halof:halof + rows, cols] = _dot(u2, w_up_ref[:, dff + c0:dff + c0 + FFN_COLS])
        hg = 0.5 * cbf_ref[:, cols]
        for s in range(kc):
            off = halof - s * SUBLANES
            hg = hg + (0.5 * cwf_ref[kc - 1 - s:kc - s, cols]) * g_buf[off:off + rows, cols]
        v = _dot(u2, w_up_ref[:, cols])
        act_scr[:, cols] = (_silu_from_half(hg) * v).astype(bf16)

    def ffn_down(k0, k1):
        part = _dot(act_scr[:, k0:k1], w_down_ref[k0:k1, :])
        if k0 == 0:
            y2_scr[...] = part
        else:
            y2_scr[...] += part

    def ffn_out(k0):
        y = y2_scr[...] + _dot(act_scr[:, k0:], w_down_ref[k0:, :])
        z = res2_scr[...] + _scale_shift(y, 1.0 + modp_ref[5])
        res = _layer_norm_rows(z, ln2g_ref[...], ln2b_ref[...])
        for c in range(d // LANES):
            res_tb[c] = res[:, c * LANES:(c + 1) * LANES]
        g_buf[0:halof, :] = g_buf[rows:rows + halof, :]

    def from_time_major():
        for b in range(SUBLANES):
            for c in range(d // LANES):
                o_ref[b, :, c * LANES:(c + 1) * LANES] = res_tb[c, pl.ds(b, tt, stride=SUBLANES), :]

    chunk = lambda n: functools.partial(ffn_chunk, n * FFN_COLS)
    down = lambda n0, n1: functools.partial(ffn_down, n0 * FFN_COLS, n1 * FFN_COLS)
    conv = lambda n: functools.partial(mix_conv, n * conv_rows)
    gates = lambda j: functools.partial(mix_gates, j)
    head = lambda j: functools.partial(mix_head_norm, j)
    assert dff == 11 * FFN_COLS and tt == 8 * CONV_STEPS

    schedule = [
        ffn_in, chunk(0),
        to_time_major, in_mod, in_proj_xa, in_proj_ga, in_proj_glu,
        chunk(1), gates(0), gates(1), conv(0), conv(1),
        chunk(2), mix_scan, mix_gelu, conv(2), conv(3),
        chunk(3), down(0, 3), mix_out_lru, conv(4), conv(5),
        chunk(4), conv(6), conv(7),
        chunk(5), head(0),
        chunk(6), down(3, 6), head(1),
        chunk(7), mix_out, move_history,
        chunk(8), chunk(9), down(6, 9), chunk(10),
        functools.partial(ffn_out, 9 * FFN_COLS), from_time_major,
    ]
    for piece in schedule:
        piece()


def _block(x, mod, mixer_consts, ffn_consts, *, alpha, tt):
    bsz, seq, d = x.shape
    (w_in, cw4, cb4, wg, br, bi, lam, cw31, cb31, ng, nb, mavg, w_out, ln1g, ln1b) = mixer_consts
    (w_up, cwf, cbf, w_down, ln2g, ln2b) = ffn_consts
    k4, lw = cw4.shape
    k31 = cw31.shape[0]
    kc, dff = cwf.shape
    nt = seq // tt
    n_blocks = (bsz // SUBLANES) * nt
    rows = tt * SUBLANES
    assert seq % tt == 0 and bsz % SUBLANES == 0 and d % LANES == 0 and dff % FFN_COLS == 0
    assert k4 - 1 <= tt and k31 - 1 <= tt and kc - 1 <= tt
    cw31 = jnp.broadcast_to(cw31[:, None, :], (k31, SUBLANES, lw))
    mod = mod.reshape(bsz, -1, d).transpose(1, 0, 2)
    consts = (w_in, cw4, cb4, wg, br, bi, lam, cw31, cb31, ng, nb, mavg, w_out, ln1g, ln1b,
              w_up, cwf, cbf, w_down, ln2g, ln2b)
    kern = functools.partial(_block_kernel, alpha=alpha, tt=tt, nt=nt, lw=lw, k4=k4, k31=k31,
                             dff=dff, kc=kc)

    def lagged(lag):
        return lambda i: jnp.clip(i - lag, 0, n_blocks - 1)

    def x_spec(lag):
        blk = lagged(lag)
        return pl.BlockSpec((SUBLANES, tt, d), lambda i: (blk(i) // nt, blk(i) % nt, 0))

    def mod_spec(lag):
        blk = lagged(lag)
        return pl.BlockSpec((mod.shape[0], SUBLANES, d), lambda i: (0, blk(i) // nt, 0))

    def const_spec(a):
        return pl.BlockSpec(a.shape, lambda i: (0,) * a.ndim, pipeline_mode=pl.Buffered(1))

    f32 = jnp.float32
    return pl.pallas_call(
        kern,
        grid=(n_blocks + 1,),
        in_specs=[x_spec(0), mod_spec(0), mod_spec(1)] + [const_spec(a) for a in consts],
        out_specs=x_spec(1),
        out_shape=jax.ShapeDtypeStruct(x.shape, f32),
        scratch_shapes=[
            pltpu.VMEM((d // LANES, rows, LANES), f32),
            pltpu.VMEM((rows, d), jnp.bfloat16),
            pltpu.VMEM(((k4 - 1) * SUBLANES + rows, lw), f32),
            pltpu.VMEM((rows, lw), f32),
            pltpu.VMEM(((k31 - 1) * SUBLANES + rows, lw), f32),
            pltpu.VMEM((rows, lw), f32),
            pltpu.VMEM((rows, lw), f32),
            pltpu.VMEM((rows, lw), f32),
            pltpu.VMEM((rows, lw), f32),
            pltpu.VMEM((rows, d), jnp.bfloat16),
            pltpu.VMEM((SUBLANES, lw), f32),
            pltpu.VMEM((rows, d), f32),
            pltpu.VMEM((rows, d), jnp.bfloat16),
            pltpu.VMEM(((kc - 1) * SUBLANES + rows, dff), f32),
            pltpu.VMEM((rows, dff), jnp.bfloat16),
            pltpu.VMEM((rows, d), f32),
            pltpu.VMEM((rows, d), f32),
            pltpu.VMEM((d // LANES, rows, LANES), f32),
            pltpu.VMEM((rows, d), f32),
        ],
        compiler_params=pltpu.CompilerParams(
            dimension_semantics=("arbitrary",), vmem_limit_bytes=VMEM_LIMIT_BYTES),
        name="mixer_ffn_block",
    )(x, mod, mod, *consts)


def _block_diag_tiles(w, tile):
    h, dh, _ = w.shape
    per = tile // dh
    w = w.reshape(h // per, per, dh, dh)
    eye = jnp.eye(per, dtype=w.dtype)
    return jnp.einsum("gpij,pq->gpiqj", w, eye).reshape(h // per, tile, tile)


def kernel(x, c, w_ada, b_ada, w_in, lru_conv_w, lru_conv_b, lru_w_r, lru_b_r, lru_w_i, lru_b_i,
           lru_lambda, conv_w, conv_b, conv_norm_g, conv_norm_b, w_out, ln1_g, ln1_b, ffn_w_up,
           ffn_conv_w, ffn_conv_b, ffn_w_down, ln2_g, ln2_b):
    depth = w_ada.shape[0]
    alpha = (2 * depth) ** 0.25
    bf16 = jnp.bfloat16
    lw = lru_conv_w.shape[-1]
    conv_heads = lru_w_r.shape[1]
    tile = lw // 2
    conv_head_dim = conv_w.shape[-1] // conv_heads
    mavg = jnp.kron(jnp.eye(tile // conv_head_dim, dtype=jnp.float32),
                    jnp.full((conv_head_dim, conv_head_dim), 1.0 / conv_head_dim, jnp.float32)).astype(bf16)
    row = lambda v: v.reshape(1, -1)

    for l in range(depth):
        mod = _modulation(c, w_ada[l], b_ada[l])
        wg = jnp.concatenate([_block_diag_tiles(lru_w_r[l], tile), _block_diag_tiles(lru_w_i[l], tile)],
                             axis=-1).astype(bf16)
        mixer_consts = (w_in[l].astype(bf16), lru_conv_w[l], row(lru_conv_b[l]), wg, row(lru_b_r[l]),
                        row(lru_b_i[l]), row(lru_lambda[l]), conv_w[l], row(conv_b[l]),
                        row(conv_norm_g[l]), row(conv_norm_b[l]), mavg, w_out[l].astype(bf16),
                        row(ln1_g[l]), row(ln1_b[l]))
        ffn_consts = (ffn_w_up[l].astype(bf16), ffn_conv_w[l], row(ffn_conv_b[l]),
                      ffn_w_down[l].astype(bf16), row(ln2_g[l]), row(ln2_b[l]))
        x = _block(x, mod, mixer_consts, ffn_consts, alpha=alpha, tt=TIME_STEPS)
    return x
```

```python
import functools

import jax
import jax.numpy as jnp
from jax import lax
from jax.experimental import pallas as pl
from jax.experimental.pallas import tpu as pltpu

LRU_C = 8.0
LN_EPS = 1e-5
SUBLANES = 8
LANES = 128
TIME_STEPS = 32
FFN_COLS = 256
CONV_STEPS = 4
NORM_ROWS = 64
VMEM_LIMIT_BYTES = 58 * 1024 * 1024
GELU_K = 0.7978845608028654
GELU_C = 0.044715


def _dot(a, b):
    return jnp.dot(a, b, preferred_element_type=jnp.float32)


def _sigmoid(v):
    return 0.5 * jnp.tanh(0.5 * v) + 0.5


def _silu_from_half(h):
    return h + h * jnp.tanh(h)


def _layer_norm_rows(z, g, b):
    mu = jnp.mean(z, axis=-1, keepdims=True)
    d = z - mu
    var = jnp.mean(d * d, axis=-1, keepdims=True)
    return d * lax.rsqrt(var + LN_EPS) * g + b


def _scale_shift(v, scale_tile, shift_tile=None):
    v3 = v.reshape(v.shape[0] // SUBLANES, SUBLANES, v.shape[1]) * scale_tile[None]
    if shift_tile is not None:
        v3 = v3 + shift_tile[None]
    return v3.reshape(v.shape)


def _mod_kernel(c_ref, w_ref, b_ref, o_ref):
    c = c_ref[...]
    c_act = c * _sigmoid(c)
    o_ref[...] = _dot(c_act, w_ref[...]) + b_ref[...]


def _modulation(c, w_ada, b_ada):
    bsz, d = c.shape
    n = w_ada.shape[1]
    return pl.pallas_call(
        _mod_kernel,
        grid=(n // d,),
        in_specs=[
            pl.BlockSpec((bsz, d), lambda j: (0, 0)),
            pl.BlockSpec((d, d), lambda j: (0, j)),
            pl.BlockSpec((1, d), lambda j: (0, j)),
        ],
        out_specs=pl.BlockSpec((bsz, d), lambda j: (0, j)),
        out_shape=jax.ShapeDtypeStruct((bsz, n), jnp.float32),
        compiler_params=pltpu.CompilerParams(
            dimension_semantics=("arbitrary",), vmem_limit_bytes=VMEM_LIMIT_BYTES),
        name="adaln_modulation",
    )(c, w_ada, b_ada.reshape(1, n))


def _block_kernel(x_ref, mod_ref, modp_ref,
                  w_in_ref, cw4_ref, cb4_ref, wg_ref, br_ref, bi_ref, lam_ref,
                  cw31_ref, cb31_ref, ng_ref, nb_ref, mavg_ref, w_out_ref, ln1g_ref, ln1b_ref,
                  w_up_ref, cwf_ref, cbf_ref, w_down_ref, ln2g_ref, ln2b_ref,
                  o_ref,
                  x_tb, u_scr, xa_buf, ga_scr, vb_buf, a_scr, b_scr, vc_scr, y_scr,
                  carry_scr, x1_scr, u2_scr, g_hist, act_scr, y1_scr, y2_scr, res_tb,
                  res2_scr,
                  *, alpha, tt, nt, lw, k4, k31, dff, kc):
    i = pl.program_id(0)
    t_mix = lax.rem(i, nt)
    t_ffn = lax.rem(i + nt - 1, nt)
    rows = tt * SUBLANES
    d = x1_scr.shape[1]
    half = lw // 2
    bf16 = jnp.bfloat16
    halo4 = (k4 - 1) * SUBLANES
    halo31 = (k31 - 1) * SUBLANES
    halof = (kc - 1) * SUBLANES

    @pl.when(i == 0)
    def _():
        x1_scr[...] = jnp.zeros_like(x1_scr)

    @pl.when(t_mix == 0)
    def _():
        xa_buf[0:halo4, :] = jnp.zeros((halo4, lw), jnp.float32)
        vb_buf[0:halo31, :] = jnp.zeros((halo31, lw), jnp.float32)
        carry_scr[...] = jnp.zeros_like(carry_scr)

    @pl.when(jnp.logical_or(t_ffn == 0, i == 0))
    def _():
        g_hist[...] = jnp.zeros_like(g_hist)

    def time_major(ref3):
        return jnp.concatenate([ref3[c] for c in range(d // LANES)], axis=1)


    def to_time_major():
        for b in range(SUBLANES):
            for c in range(d // LANES):
                x_tb[c, pl.ds(b, tt, stride=SUBLANES), :] = x_ref[b, :, c * LANES:(c + 1) * LANES]

    def in_mod():
        u_scr[...] = _scale_shift(time_major(x_tb), 1.0 + mod_ref[1], mod_ref[0]).astype(bf16)

    def in_proj_xa():
        xa_buf[halo4:halo4 + rows, :] = _dot(u_scr[...], w_in_ref[:, 0:lw])

    def in_proj_ga():
        ga_scr[...] = _dot(u_scr[...], w_in_ref[:, lw:2 * lw])

    def in_proj_glu():
        u = u_scr[...]
        vb = _dot(u, w_in_ref[:, 2 * lw:3 * lw])
        gb = _dot(u, w_in_ref[:, 3 * lw:4 * lw])
        vb_buf[halo31:halo31 + rows, :] = vb * _sigmoid(gb)

    def move_history():
        xa_buf[0:halo4, :] = xa_buf[rows:rows + halo4, :]
        vb_buf[0:halo31, :] = vb_buf[rows:rows + halo31, :]

    def mix_gates(j):
        cols = slice(j * half, (j + 1) * half)
        xh = cb4_ref[:, cols]
        for s in range(k4):
            off = halo4 - s * SUBLANES
            xh = xh + cw4_ref[k4 - 1 - s:k4 - s, cols] * xa_buf[off:off + rows, cols]
        gates = _dot(xh.astype(bf16), wg_ref[j])
        r = _sigmoid(gates[:, 0:half] + br_ref[:, cols])
        gi = _sigmoid(gates[:, half:2 * half] + bi_ref[:, cols])
        log_a = r * (-LRU_C * jax.nn.softplus(-lam_ref[:, cols]))
        a = jnp.exp(log_a)
        mult = jnp.sqrt(-jnp.tanh(log_a) * (a * a + 1.0))
        a_scr[:, cols] = a
        b_scr[:, cols] = mult * (gi * xh)

    def mix_scan():
        h = carry_scr[...]
        for t in range(tt):
            r0 = t * SUBLANES
            h = a_scr[r0:r0 + SUBLANES, :] * h + b_scr[r0:r0 + SUBLANES, :]
            b_scr[r0:r0 + SUBLANES, :] = h
        carry_scr[...] = h

    def mix_gelu():
        ga = ga_scr[...]
        inner = ga * (ga * ga * (GELU_K * GELU_C) + GELU_K)
        hx = 0.5 * ga
        y_scr[:, 0:lw] = ((hx + hx * jnp.tanh(inner)) * b_scr[...]).astype(bf16)

    conv_rows = CONV_STEPS * SUBLANES
    groups = (CONV_STEPS, SUBLANES, lw)

    def mix_conv(c0):
        acc = jnp.broadcast_to(cb31_ref[...].reshape(1, 1, lw), groups)
        for s in range(k31):
            w_tile = cw31_ref[k31 - 1 - s]
            off = halo31 - s * SUBLANES + c0
            acc = acc + w_tile[None] * vb_buf[off:off + conv_rows, :].reshape(groups)
        vc_scr[c0:c0 + conv_rows, :] = acc.reshape(conv_rows, lw)

    def mix_head_norm(j):
        cols = slice(j * half, (j + 1) * half)
        vc = vc_scr[:, cols]
        dv = vc - _dot(vc.astype(bf16), mavg_ref[...])
        var = _dot((dv * dv).astype(bf16), mavg_ref[...])
        hn = dv * lax.rsqrt(var + LN_EPS) * (0.5 * ng_ref[:, cols]) + 0.5 * nb_ref[:, cols]
        y_scr[:, lw + j * half:lw + (j + 1) * half] = _silu_from_half(hn).astype(bf16)

    def mix_out_lru():
        y1_scr[...] = _dot(y_scr[:, 0:lw], w_out_ref[0:lw, :])

    def mix_out():
        part = _dot(y_scr[:, lw:], w_out_ref[lw:, :])
        gate = 1.0 + mod_ref[2]
        for r0 in range(0, rows, NORM_ROWS):
            rs = slice(r0, r0 + NORM_ROWS)
            x_rows = jnp.concatenate([x_tb[c, rs, :] for c in range(d // LANES)], axis=1)
            z = alpha * x_rows + _scale_shift(y1_scr[rs, :] + part[rs, :], gate)
            x1_scr[rs, :] = _layer_norm_rows(z, ln1g_ref[...], ln1b_ref[...])


    def ffn_in():
        x1 = x1_scr[...]
        u2_scr[...] = _scale_shift(x1, 1.0 + modp_ref[4], modp_ref[3]).astype(bf16)
        res2_scr[...] = alpha * x1

    def ffn_chunk(c0):
        cols = slice(c0, c0 + FFN_COLS)
        u2 = u2_scr[...]
        g = _dot(u2, w_up_ref[:, dff + c0:dff + c0 + FFN_COLS])
        g_ext = jnp.concatenate([g_hist[:, cols], g], axis=0)
        g_hist[:, cols] = g[rows - halof:, :]
        hg = 0.5 * cbf_ref[:, cols]
        for s in range(kc):
            off = halof - s * SUBLANES
            hg = hg + (0.5 * cwf_ref[kc - 1 - s:kc - s, cols]) * g_ext[off:off + rows, :]
        v = _dot(u2, w_up_ref[:, cols])
        act_scr[:, cols] = (_silu_from_half(hg) * v).astype(bf16)

    def ffn_down(k0, k1):
        part = _dot(act_scr[:, k0:k1], w_down_ref[k0:k1, :])
        if k0 == 0:
            y2_scr[...] = part
        else:
            y2_scr[...] += part

    def ffn_out(k0):
        part = _dot(act_scr[:, k0:], w_down_ref[k0:, :])
        gate = 1.0 + modp_ref[5]
        for r0 in range(0, rows, NORM_ROWS):
            rs = slice(r0, r0 + NORM_ROWS)
            z = res2_scr[rs, :] + _scale_shift(y2_scr[rs, :] + part[rs, :], gate)
            res = _layer_norm_rows(z, ln2g_ref[...], ln2b_ref[...])
            for c in range(d // LANES):
                res_tb[c, rs, :] = res[:, c * LANES:(c + 1) * LANES]

    def from_time_major():
        for b in range(SUBLANES):
            for c in range(d // LANES):
                o_ref[b, :, c * LANES:(c + 1) * LANES] = res_tb[c, pl.ds(b, tt, stride=SUBLANES), :]

    chunk = lambda n: functools.partial(ffn_chunk, n * FFN_COLS)
    down = lambda n0, n1: functools.partial(ffn_down, n0 * FFN_COLS, n1 * FFN_COLS)
    conv = lambda n: functools.partial(mix_conv, n * conv_rows)
    gates = lambda j: functools.partial(mix_gates, j)
    head = lambda j: functools.partial(mix_head_norm, j)
    assert dff == 11 * FFN_COLS and tt % (4 * CONV_STEPS) == 0
    per = tt // (4 * CONV_STEPS)
    convs = lambda q: [conv(n) for n in range(q * per, (q + 1) * per)]

    schedule = [
        ffn_in, chunk(0),
        to_time_major, in_mod, in_proj_xa, chunk(1), in_proj_glu, chunk(2), in_proj_ga,
        gates(0), gates(1), chunk(3), *convs(0),
        mix_scan, mix_gelu, chunk(4), *convs(1),
        down(0, 3), chunk(5), *convs(2), mix_out_lru,
        chunk(6), *convs(3),
        chunk(7), down(3, 6), head(0),
        chunk(8), head(1),
        chunk(9), mix_out, move_history,
        down(6, 9), chunk(10),
        functools.partial(ffn_out, 9 * FFN_COLS), from_time_major,
    ]
    for piece in schedule:
        piece()


def _block(x, mod, mixer_consts, ffn_consts, *, alpha, tt):
    bsz, seq, d = x.shape
    (w_in, cw4, cb4, wg, br, bi, lam, cw31, cb31, ng, nb, mavg, w_out, ln1g, ln1b) = mixer_consts
    (w_up, cwf, cbf, w_down, ln2g, ln2b) = ffn_consts
    k4, lw = cw4.shape
    k31 = cw31.shape[0]
    kc, dff = cwf.shape
    nt = seq // tt
    n_blocks = (bsz // SUBLANES) * nt
    rows = tt * SUBLANES
    assert seq % tt == 0 and bsz % SUBLANES == 0 and d % LANES == 0 and dff % FFN_COLS == 0
    assert k4 - 1 <= tt and k31 - 1 <= tt and kc - 1 <= tt
    cw31 = jnp.broadcast_to(cw31[:, None, :], (k31, SUBLANES, lw))
    mod = mod.reshape(bsz, -1, d).transpose(1, 0, 2)
    consts = (w_in, cw4, cb4, wg, br, bi, lam, cw31, cb31, ng, nb, mavg, w_out, ln1g, ln1b,
              w_up, cwf, cbf, w_down, ln2g, ln2b)
    kern = functools.partial(_block_kernel, alpha=alpha, tt=tt, nt=nt, lw=lw, k4=k4, k31=k31,
                             dff=dff, kc=kc)

    def lagged(lag):
        return lambda i: jnp.clip(i - lag, 0, n_blocks - 1)

    def x_spec(lag):
        blk = lagged(lag)
        return pl.BlockSpec((SUBLANES, tt, d), lambda i: (blk(i) // nt, blk(i) % nt, 0))

    def mod_spec(lag):
        blk = lagged(lag)
        return pl.BlockSpec((mod.shape[0], SUBLANES, d), lambda i: (0, blk(i) // nt, 0))

    def const_spec(a):
        return pl.BlockSpec(a.shape, lambda i: (0,) * a.ndim, pipeline_mode=pl.Buffered(1))

    f32 = jnp.float32
    return pl.pallas_call(
        kern,
        grid=(n_blocks + 1,),
        in_specs=[x_spec(0), mod_spec(0), mod_spec(1)] + [const_spec(a) for a in consts],
        out_specs=x_spec(1),
        out_shape=jax.ShapeDtypeStruct(x.shape, f32),
        scratch_shapes=[
            pltpu.VMEM((d // LANES, rows, LANES), f32),
            pltpu.VMEM((rows, d), jnp.bfloat16),
            pltpu.VMEM(((k4 - 1) * SUBLANES + rows, lw), f32),
            pltpu.VMEM((rows, lw), f32),
            pltpu.VMEM(((k31 - 1) * SUBLANES + rows, lw), f32),
            pltpu.VMEM((rows, lw), f32),
            pltpu.VMEM((rows, lw), f32),
            pltpu.VMEM((rows, lw), f32),
            pltpu.VMEM((rows, d), jnp.bfloat16),
            pltpu.VMEM((SUBLANES, lw), f32),
            pltpu.VMEM((rows, d), f32),
            pltpu.VMEM((rows, d), jnp.bfloat16),
            pltpu.VMEM(((kc - 1) * SUBLANES, dff), f32),
            pltpu.VMEM((rows, dff), jnp.bfloat16),
            pltpu.VMEM((rows, d), f32),
            pltpu.VMEM((rows, d), f32),
            pltpu.VMEM((d // LANES, rows, LANES), f32),
            pltpu.VMEM((rows, d), f32),
        ],
        compiler_params=pltpu.CompilerParams(
            dimension_semantics=("arbitrary",), vmem_limit_bytes=VMEM_LIMIT_BYTES),
        name="mixer_ffn_block",
    )(x, mod, mod, *consts)


def _block_diag_tiles(w, tile):
    h, dh, _ = w.shape
    per = tile // dh
    w = w.reshape(h // per, per, dh, dh)
    eye = jnp.eye(per, dtype=w.dtype)
    return jnp.einsum("gpij,pq->gpiqj", w, eye).reshape(h // per, tile, tile)


def kernel(x, c, w_ada, b_ada, w_in, lru_conv_w, lru_conv_b, lru_w_r, lru_b_r, lru_w_i, lru_b_i,
           lru_lambda, conv_w, conv_b, conv_norm_g, conv_norm_b, w_out, ln1_g, ln1_b, ffn_w_up,
           ffn_conv_w, ffn_conv_b, ffn_w_down, ln2_g, ln2_b):
    depth = w_ada.shape[0]
    alpha = (2 * depth) ** 0.25
    bf16 = jnp.bfloat16
    lw = lru_conv_w.shape[-1]
    conv_heads = lru_w_r.shape[1]
    tile = lw // 2
    conv_head_dim = conv_w.shape[-1] // conv_heads
    mavg = jnp.kron(jnp.eye(tile // conv_head_dim, dtype=jnp.float32),
                    jnp.full((conv_head_dim, conv_head_dim), 1.0 / conv_head_dim, jnp.float32)).astype(bf16)
    row = lambda v: v.reshape(1, -1)

    for l in range(depth):
        mod = _modulation(c, w_ada[l], b_ada[l])
        wg = jnp.concatenate([_block_diag_tiles(lru_w_r[l], tile), _block_diag_tiles(lru_w_i[l], tile)],
                             axis=-1).astype(bf16)
        mixer_consts = (w_in[l].astype(bf16), lru_conv_w[l], row(lru_conv_b[l]), wg, row(lru_b_r[l]),
                        row(lru_b_i[l]), row(lru_lambda[l]), conv_w[l], row(conv_b[l]),
                        row(conv_norm_g[l]), row(conv_norm_b[l]), mavg, w_out[l].astype(bf16),
                        row(ln1_g[l]), row(ln1_b[l]))
        ffn_consts = (ffn_w_up[l].astype(bf16), ffn_conv_w[l], row(ffn_conv_b[l]),
                      ffn_w_down[l].astype(bf16), row(ln2_g[l]), row(ln2_b[l]))
        x = _block(x, mod, mixer_consts, ffn_consts, alpha=alpha, tt=TIME_STEPS)
    return x
```

```python
import functools

import jax
import jax.numpy as jnp
from jax import lax
from jax.experimental import pallas as pl
from jax.experimental.pallas import tpu as pltpu

LRU_C = 8.0
LN_EPS = 1e-5
SUBLANES = 8
LANES = 128
TIME_STEPS = 32
FFN_COLS = 256
CONV_STEPS = 8
NORM_ROWS = 64
VMEM_LIMIT_BYTES = 58 * 1024 * 1024
GELU_K = 0.7978845608028654
GELU_C = 0.044715


def _dot(a, b):
    return jnp.dot(a, b, preferred_element_type=jnp.float32)


def _sigmoid(v):
    return 0.5 * jnp.tanh(0.5 * v) + 0.5


def _silu_from_half(h):
    return h + h * jnp.tanh(h)


def _layer_norm_rows(z, g, b):
    mu = jnp.mean(z, axis=-1, keepdims=True)
    d = z - mu
    var = jnp.mean(d * d, axis=-1, keepdims=True)
    return d * lax.rsqrt(var + LN_EPS) * g + b


def _scale_shift(v, scale_tile, shift_tile=None):
    v3 = v.reshape(v.shape[0] // SUBLANES, SUBLANES, v.shape[1]) * scale_tile[None]
    if shift_tile is not None:
        v3 = v3 + shift_tile[None]
    return v3.reshape(v.shape)


def _mod_kernel(c_ref, w_ref, b_ref, o_ref):
    c = c_ref[...]
    c_act = c * _sigmoid(c)
    o_ref[...] = _dot(c_act, w_ref[...]) + b_ref[...]


def _modulation(c, w_ada, b_ada):
    bsz, d = c.shape
    n = w_ada.shape[1]
    return pl.pallas_call(
        _mod_kernel,
        grid=(n // d,),
        in_specs=[
            pl.BlockSpec((bsz, d), lambda j: (0, 0)),
            pl.BlockSpec((d, d), lambda j: (0, j)),
            pl.BlockSpec((1, d), lambda j: (0, j)),
        ],
        out_specs=pl.BlockSpec((bsz, d), lambda j: (0, j)),
        out_shape=jax.ShapeDtypeStruct((bsz, n), jnp.float32),
        compiler_params=pltpu.CompilerParams(
            dimension_semantics=("arbitrary",), vmem_limit_bytes=VMEM_LIMIT_BYTES),
        name="adaln_modulation",
    )(c, w_ada, b_ada.reshape(1, n))


def _block_kernel(x_ref, mod_ref, modp_ref,
                  w_in_ref, cw4_ref, cb4_ref, wg_ref, br_ref, bi_ref, lam_ref,
                  cw31_ref, cb31_ref, ng_ref, nb_ref, mavg_ref, w_out_ref, ln1g_ref, ln1b_ref,
                  w_up_ref, cwf_ref, cbf_ref, w_down_ref, ln2g_ref, ln2b_ref,
                  o_ref,
                  x_tb, u_scr, xa_buf, ga_scr, vb_buf, a_scr, b_scr, vc_scr, y_scr,
                  carry_scr, x1_scr, u2_scr, g_hist, act_scr, y1_scr, y2_scr, res_tb,
                  res2_scr,
                  *, alpha, tt, nt, lw, k4, k31, dff, kc):
    i = pl.program_id(0)
    t_mix = lax.rem(i, nt)
    t_ffn = lax.rem(i + nt - 1, nt)
    rows = tt * SUBLANES
    d = x1_scr.shape[1]
    half = lw // 2
    bf16 = jnp.bfloat16
    halo4 = (k4 - 1) * SUBLANES
    halo31 = (k31 - 1) * SUBLANES
    halof = (kc - 1) * SUBLANES

    @pl.when(i == 0)
    def _():
        x1_scr[...] = jnp.zeros_like(x1_scr)

    @pl.when(t_mix == 0)
    def _():
        xa_buf[0:halo4, :] = jnp.zeros((halo4, lw), jnp.float32)
        vb_buf[0:halo31, :] = jnp.zeros((halo31, lw), jnp.float32)
        carry_scr[...] = jnp.zeros_like(carry_scr)

    @pl.when(jnp.logical_or(t_ffn == 0, i == 0))
    def _():
        g_hist[...] = jnp.zeros_like(g_hist)

    def time_major(ref3):
        return jnp.concatenate([ref3[c] for c in range(d // LANES)], axis=1)


    def to_time_major():
        for b in range(SUBLANES):
            for c in range(d // LANES):
                x_tb[c, pl.ds(b, tt, stride=SUBLANES), :] = x_ref[b, :, c * LANES:(c + 1) * LANES]

    def in_mod():
        u_scr[...] = _scale_shift(time_major(x_tb), 1.0 + mod_ref[1], mod_ref[0]).astype(bf16)

    def in_proj_xa():
        xa_buf[halo4:halo4 + rows, :] = _dot(u_scr[...], w_in_ref[:, 0:lw])

    def in_proj_ga():
        ga_scr[...] = _dot(u_scr[...], w_in_ref[:, lw:2 * lw])

    def in_proj_glu():
        u = u_scr[...]
        hv = _dot(u, w_in_ref[:, 2 * lw:3 * lw])
        hg = _dot(u, w_in_ref[:, 3 * lw:4 * lw])
        vb_buf[halo31:halo31 + rows, :] = hv + hv * jnp.tanh(hg)

    def move_history():
        xa_buf[0:halo4, :] = xa_buf[rows:rows + halo4, :]
        vb_buf[0:halo31, :] = vb_buf[rows:rows + halo31, :]

    def mix_gates(j):
        cols = slice(j * half, (j + 1) * half)
        xh = cb4_ref[:, cols]
        for s in range(k4):
            off = halo4 - s * SUBLANES
            xh = xh + cw4_ref[k4 - 1 - s:k4 - s, cols] * xa_buf[off:off + rows, cols]
        gates = _dot(xh.astype(bf16), wg_ref[j])
        half_scale = -0.5 * LRU_C * jax.nn.softplus(-lam_ref[:, cols])
        log_a = jnp.tanh(gates[:, 0:half] + 0.5 * br_ref[:, cols]) * half_scale + half_scale
        gi = 0.5 * jnp.tanh(gates[:, half:2 * half] + 0.5 * bi_ref[:, cols]) + 0.5
        a = jnp.exp(log_a)
        m2 = -jnp.tanh(log_a) * (a * a + 1.0)
        mult = jnp.where(m2 > 0.0, m2 * lax.rsqrt(m2), 0.0)
        a_scr[:, cols] = a
        b_scr[:, cols] = mult * (gi * xh)

    def mix_scan():
        h = carry_scr[...]
        for t in range(tt):
            r0 = t * SUBLANES
            h = a_scr[r0:r0 + SUBLANES, :] * h + b_scr[r0:r0 + SUBLANES, :]
            b_scr[r0:r0 + SUBLANES, :] = h
        carry_scr[...] = h

    def mix_gelu():
        hx = ga_scr[...]
        inner = hx * (hx * hx * (8.0 * GELU_K * GELU_C) + 2.0 * GELU_K)
        y_scr[:, 0:lw] = ((hx + hx * jnp.tanh(inner)) * b_scr[...]).astype(bf16)

    conv_rows = CONV_STEPS * SUBLANES
    groups = (CONV_STEPS, SUBLANES, LANES)

    def mix_conv(q):
        lanes = slice(q * LANES, (q + 1) * LANES)
        taps = [cw31_ref[k31 - 1 - s, :, lanes] for s in range(k31)]
        for c0 in range(0, rows, conv_rows):
            acc = jnp.broadcast_to(cb31_ref[:, lanes].reshape(1, 1, LANES), groups)
            for s in range(k31):
                off = halo31 - s * SUBLANES + c0
                acc = acc + taps[s][None] * vb_buf[off:off + conv_rows, lanes].reshape(groups)
            vc_scr[c0:c0 + conv_rows, lanes] = acc.reshape(conv_rows, LANES)

    def mix_head_norm(j):
        cols = slice(j * half, (j + 1) * half)
        vc = vc_scr[:, cols]
        dv = vc - _dot(vc.astype(bf16), mavg_ref[...])
        var = _dot((dv * dv).astype(bf16), mavg_ref[...])
        hn = dv * lax.rsqrt(var + LN_EPS) * (0.5 * ng_ref[:, cols]) + 0.5 * nb_ref[:, cols]
        y_scr[:, lw + j * half:lw + (j + 1) * half] = _silu_from_half(hn).astype(bf16)

    def mix_out_lru():
        y1_scr[...] = _dot(y_scr[:, 0:lw], w_out_ref[0:lw, :])

    def mix_out():
        part = _dot(y_scr[:, lw:], w_out_ref[lw:, :])
        gate = 1.0 + mod_ref[2]
        for r0 in range(0, rows, NORM_ROWS):
            rs = slice(r0, r0 + NORM_ROWS)
            x_rows = jnp.concatenate([x_tb[c, rs, :] for c in range(d // LANES)], axis=1)
            z = alpha * x_rows + _scale_shift(y1_scr[rs, :] + part[rs, :], gate)
            x1_scr[rs, :] = _layer_norm_rows(z, ln1g_ref[...], ln1b_ref[...])


    def ffn_in():
        x1 = x1_scr[...]
        u2_scr[...] = _scale_shift(x1, 1.0 + modp_ref[4], modp_ref[3]).astype(bf16)
        res2_scr[...] = alpha * x1

    def ffn_chunk(c0):
        cols = slice(c0, c0 + FFN_COLS)
        u2 = u2_scr[...]
        g = _dot(u2, w_up_ref[:, dff + c0:dff + c0 + FFN_COLS])
        g_ext = jnp.concatenate([g_hist[:, cols], g], axis=0)
        g_hist[:, cols] = g[rows - halof:, :]
        hg = 0.5 * cbf_ref[:, cols]
        for s in range(kc):
            off = halof - s * SUBLANES
            hg = hg + (0.5 * cwf_ref[kc - 1 - s:kc - s, cols]) * g_ext[off:off + rows, :]
        v = _dot(u2, w_up_ref[:, cols])
        act_scr[:, cols] = (_silu_from_half(hg) * v).astype(bf16)

    def ffn_down(k0, k1):
        part = _dot(act_scr[:, k0:k1], w_down_ref[k0:k1, :])
        if k0 == 0:
            y2_scr[...] = part
        else:
            y2_scr[...] += part

    def ffn_out(k0):
        part = _dot(act_scr[:, k0:], w_down_ref[k0:, :])
        gate = 1.0 + modp_ref[5]
        for r0 in range(0, rows, NORM_ROWS):
            rs = slice(r0, r0 + NORM_ROWS)
            z = res2_scr[rs, :] + _scale_shift(y2_scr[rs, :] + part[rs, :], gate)
            res = _layer_norm_rows(z, ln2g_ref[...], ln2b_ref[...])
            for c in range(d // LANES):
                res_tb[c, rs, :] = res[:, c * LANES:(c + 1) * LANES]

    def from_time_major():
        for b in range(SUBLANES):
            for c in range(d // LANES):
                o_ref[b, :, c * LANES:(c + 1) * LANES] = res_tb[c, pl.ds(b, tt, stride=SUBLANES), :]

    chunk = lambda n: functools.partial(ffn_chunk, n * FFN_COLS)
    down = lambda n0, n1: functools.partial(ffn_down, n0 * FFN_COLS, n1 * FFN_COLS)
    conv = lambda q: functools.partial(mix_conv, q)
    gates = lambda j: functools.partial(mix_gates, j)
    head = lambda j: functools.partial(mix_head_norm, j)
    assert dff == 11 * FFN_COLS and tt % CONV_STEPS == 0 and lw == 4 * LANES
    convs = lambda q: [conv(q)]

    schedule = [
        ffn_in, chunk(0),
        to_time_major, in_mod, in_proj_xa, chunk(1), in_proj_glu, chunk(2), in_proj_ga,
        gates(0), gates(1), chunk(3), *convs(0),
        mix_scan, mix_gelu, chunk(4), *convs(1),
        down(0, 3), chunk(5), *convs(2), mix_out_lru,
        chunk(6), *convs(3),
        chunk(7), down(3, 6), head(0),
        chunk(8), head(1),
        chunk(9), mix_out, move_history,
        down(6, 9), chunk(10),
        functools.partial(ffn_out, 9 * FFN_COLS), from_time_major,
    ]
    for piece in schedule:
        piece()


def _block(x, mod, mixer_consts, ffn_consts, *, alpha, tt):
    bsz, seq, d = x.shape
    (w_in, cw4, cb4, wg, br, bi, lam, cw31, cb31, ng, nb, mavg, w_out, ln1g, ln1b) = mixer_consts
    (w_up, cwf, cbf, w_down, ln2g, ln2b) = ffn_consts
    k4, lw = cw4.shape
    k31 = cw31.shape[0]
    kc, dff = cwf.shape
    nt = seq // tt
    n_blocks = (bsz // SUBLANES) * nt
    rows = tt * SUBLANES
    assert seq % tt == 0 and bsz % SUBLANES == 0 and d % LANES == 0 and dff % FFN_COLS == 0
    assert k4 - 1 <= tt and k31 - 1 <= tt and kc - 1 <= tt
    cw31 = jnp.broadcast_to(cw31[:, None, :], (k31, SUBLANES, lw))
    mod = mod.reshape(bsz, -1, d).transpose(1, 0, 2)
    consts = (w_in, cw4, cb4, wg, br, bi, lam, cw31, cb31, ng, nb, mavg, w_out, ln1g, ln1b,
              w_up, cwf, cbf, w_down, ln2g, ln2b)
    kern = functools.partial(_block_kernel, alpha=alpha, tt=tt, nt=nt, lw=lw, k4=k4, k31=k31,
                             dff=dff, kc=kc)

    def lagged(lag):
        return lambda i: jnp.clip(i - lag, 0, n_blocks - 1)

    def x_spec(lag):
        blk = lagged(lag)
        return pl.BlockSpec((SUBLANES, tt, d), lambda i: (blk(i) // nt, blk(i) % nt, 0))

    def mod_spec(lag):
        blk = lagged(lag)
        return pl.BlockSpec((mod.shape[0], SUBLANES, d), lambda i: (0, blk(i) // nt, 0))

    def const_spec(a):
        return pl.BlockSpec(a.shape, lambda i: (0,) * a.ndim, pipeline_mode=pl.Buffered(1))

    f32 = jnp.float32
    return pl.pallas_call(
        kern,
        grid=(n_blocks + 1,),
        in_specs=[x_spec(0), mod_spec(0), mod_spec(1)] + [const_spec(a) for a in consts],
        out_specs=x_spec(1),
        out_shape=jax.ShapeDtypeStruct(x.shape, f32),
        scratch_shapes=[
            pltpu.VMEM((d // LANES, rows, LANES), f32),
            pltpu.VMEM((rows, d), jnp.bfloat16),
            pltpu.VMEM(((k4 - 1) * SUBLANES + rows, lw), f32),
            pltpu.VMEM((rows, lw), f32),
            pltpu.VMEM(((k31 - 1) * SUBLANES + rows, lw), f32),
            pltpu.VMEM((rows, lw), f32),
            pltpu.VMEM((rows, lw), f32),
            pltpu.VMEM((rows, lw), f32),
            pltpu.VMEM((rows, d), jnp.bfloat16),
            pltpu.VMEM((SUBLANES, lw), f32),
            pltpu.VMEM((rows, d), f32),
            pltpu.VMEM((rows, d), jnp.bfloat16),
            pltpu.VMEM(((kc - 1) * SUBLANES, dff), f32),
            pltpu.VMEM((rows, dff), jnp.bfloat16),
            pltpu.VMEM((rows, d), f32),
            pltpu.VMEM((rows, d), f32),
            pltpu.VMEM((d // LANES, rows, LANES), f32),
            pltpu.VMEM((rows, d), f32),
        ],
        compiler_params=pltpu.CompilerParams(
            dimension_semantics=("arbitrary",), vmem_limit_bytes=VMEM_LIMIT_BYTES),
        name="mixer_ffn_block",
    )(x, mod, mod, *consts)


def _block_diag_tiles(w, tile):
    h, dh, _ = w.shape
    per = tile // dh
    w = w.reshape(h // per, per, dh, dh)
    eye = jnp.eye(per, dtype=w.dtype)
    return jnp.einsum("gpij,pq->gpiqj", w, eye).reshape(h // per, tile, tile)


def kernel(x, c, w_ada, b_ada, w_in, lru_conv_w, lru_conv_b, lru_w_r, lru_b_r, lru_w_i, lru_b_i,
           lru_lambda, conv_w, conv_b, conv_norm_g, conv_norm_b, w_out, ln1_g, ln1_b, ffn_w_up,
           ffn_conv_w, ffn_conv_b, ffn_w_down, ln2_g, ln2_b):
    depth = w_ada.shape[0]
    alpha = (2 * depth) ** 0.25
    bf16 = jnp.bfloat16
    lw = lru_conv_w.shape[-1]
    conv_heads = lru_w_r.shape[1]
    tile = lw // 2
    conv_head_dim = conv_w.shape[-1] // conv_heads
    mavg = jnp.kron(jnp.eye(tile // conv_head_dim, dtype=jnp.float32),
                    jnp.full((conv_head_dim, conv_head_dim), 1.0 / conv_head_dim, jnp.float32)).astype(bf16)
    row = lambda v: v.reshape(1, -1)

    for l in range(depth):
        mod = _modulation(c, w_ada[l], b_ada[l])
        wg = (0.5 * jnp.concatenate([_block_diag_tiles(lru_w_r[l], tile),
                                     _block_diag_tiles(lru_w_i[l], tile)], axis=-1)).astype(bf16)
        in_scale = jnp.concatenate([jnp.ones((lw,), jnp.float32), jnp.full((3 * lw,), 0.5, jnp.float32)])
        mixer_consts = ((w_in[l] * in_scale).astype(bf16), lru_conv_w[l], row(lru_conv_b[l]), wg, row(lru_b_r[l]),
                        row(lru_b_i[l]), row(lru_lambda[l]), conv_w[l], row(conv_b[l]),
                        row(conv_norm_g[l]), row(conv_norm_b[l]), mavg, w_out[l].astype(bf16),
                        row(ln1_g[l]), row(ln1_b[l]))
        ffn_consts = (ffn_w_up[l].astype(bf16), ffn_conv_w[l], row(ffn_conv_b[l]),
                      ffn_w_down[l].astype(bf16), row(ln2_g[l]), row(ln2_b[l]))
        x = _block(x, mod, mixer_consts, ffn_consts, alpha=alpha, tt=TIME_STEPS)
    return x
```

```python
import functools

import jax
import jax.numpy as jnp
from jax import lax
from jax.experimental import pallas as pl
from jax.experimental.pallas import tpu as pltpu

LRU_C = 8.0
LN_EPS = 1e-5
SUBLANES = 8
LANES = 128
TIME_STEPS = 32
FFN_COLS = 256
CONV_STEPS = 32
NORM_ROWS = 64
VMEM_LIMIT_BYTES = 58 * 1024 * 1024
GELU_K = 0.7978845608028654
GELU_C = 0.044715


def _dot(a, b):
    return jnp.dot(a, b, preferred_element_type=jnp.float32)


def _sigmoid(v):
    return 0.5 * jnp.tanh(0.5 * v) + 0.5


def _silu_from_half(h):
    return h + h * jnp.tanh(h)


def _layer_norm_rows(z, g, b):
    mu = jnp.mean(z, axis=-1, keepdims=True)
    d = z - mu
    var = jnp.mean(d * d, axis=-1, keepdims=True)
    return d * lax.rsqrt(var + LN_EPS) * g + b


def _scale_shift(v, scale_tile, shift_tile=None):
    v3 = v.reshape(v.shape[0] // SUBLANES, SUBLANES, v.shape[1]) * scale_tile[None]
    if shift_tile is not None:
        v3 = v3 + shift_tile[None]
    return v3.reshape(v.shape)


def _mod_kernel(c_ref, w_ref, b_ref, o_ref):
    c = c_ref[...]
    c_act = c * _sigmoid(c)
    o_ref[...] = _dot(c_act, w_ref[...]) + b_ref[...]


def _modulation(c, w_ada, b_ada):
    bsz, d = c.shape
    n = w_ada.shape[1]
    return pl.pallas_call(
        _mod_kernel,
        grid=(n // d,),
        in_specs=[
            pl.BlockSpec((bsz, d), lambda j: (0, 0)),
            pl.BlockSpec((d, d), lambda j: (0, j)),
            pl.BlockSpec((1, d), lambda j: (0, j)),
        ],
        out_specs=pl.BlockSpec((bsz, d), lambda j: (0, j)),
        out_shape=jax.ShapeDtypeStruct((bsz, n), jnp.float32),
        compiler_params=pltpu.CompilerParams(
            dimension_semantics=("arbitrary",), vmem_limit_bytes=VMEM_LIMIT_BYTES),
        name="adaln_modulation",
    )(c, w_ada, b_ada.reshape(1, n))


def _block_kernel(x_ref, mod_ref, modp_ref,
                  w_in_ref, cw4_ref, cb4_ref, wg_ref, br_ref, bi_ref, lam_ref,
                  cw31_ref, cb31_ref, ng_ref, nb_ref, mavg_ref, w_out_ref, ln1g_ref, ln1b_ref,
                  w_up_ref, cwf_ref, cbf_ref, w_down_ref, ln2g_ref, ln2b_ref,
                  o_ref,
                  x_tb, u_scr, xa_buf, ga_scr, vb_buf, a_scr, b_scr, vc_scr, y_scr,
                  carry_scr, x1_scr, u2_scr, g_hist, act_scr, y1_scr, y2_scr, res_tb,
                  res2_scr,
                  *, alpha, tt, nt, lw, k4, k31, dff, kc):
    i = pl.program_id(0)
    t_mix = lax.rem(i, nt)
    t_ffn = lax.rem(i + nt - 1, nt)
    rows = tt * SUBLANES
    d = x1_scr.shape[1]
    half = lw // 2
    bf16 = jnp.bfloat16
    halo4 = (k4 - 1) * SUBLANES
    halo31 = (k31 - 1) * SUBLANES
    halof = (kc - 1) * SUBLANES

    @pl.when(i == 0)
    def _():
        x1_scr[...] = jnp.zeros_like(x1_scr)

    @pl.when(t_mix == 0)
    def _():
        xa_buf[0:halo4, :] = jnp.zeros((halo4, lw), jnp.float32)
        vb_buf[0:halo31, :] = jnp.zeros((halo31, lw), jnp.float32)
        carry_scr[...] = jnp.zeros_like(carry_scr)

    @pl.when(jnp.logical_or(t_ffn == 0, i == 0))
    def _():
        g_hist[...] = jnp.zeros_like(g_hist)

    def time_major(ref3):
        return jnp.concatenate([ref3[c] for c in range(d // LANES)], axis=1)


    def to_time_major():
        for b in range(SUBLANES):
            for c in range(d // LANES):
                x_tb[c, pl.ds(b, tt, stride=SUBLANES), :] = x_ref[b, :, c * LANES:(c + 1) * LANES]

    def in_mod():
        u_scr[...] = _scale_shift(time_major(x_tb), 1.0 + mod_ref[1], mod_ref[0]).astype(bf16)

    def in_proj_xa():
        xa_buf[halo4:halo4 + rows, :] = _dot(u_scr[...], w_in_ref[:, 0:lw])

    def in_proj_ga():
        ga_scr[...] = _dot(u_scr[...], w_in_ref[:, lw:2 * lw])

    def in_proj_glu():
        u = u_scr[...]
        hv = _dot(u, w_in_ref[:, 2 * lw:3 * lw])
        hg = _dot(u, w_in_ref[:, 3 * lw:4 * lw])
        vb_buf[halo31:halo31 + rows, :] = hv + hv * jnp.tanh(hg)

    def move_history():
        xa_buf[0:halo4, :] = xa_buf[rows:rows + halo4, :]
        vb_buf[0:halo31, :] = vb_buf[rows:rows + halo31, :]

    def mix_gates(j):
        cols = slice(j * half, (j + 1) * half)
        xh = cb4_ref[:, cols]
        for s in range(k4):
            off = halo4 - s * SUBLANES
            xh = xh + cw4_ref[k4 - 1 - s:k4 - s, cols] * xa_buf[off:off + rows, cols]
        gates = _dot(xh.astype(bf16), wg_ref[j])
        half_scale = -0.5 * LRU_C * jax.nn.softplus(-lam_ref[:, cols])
        log_a = jnp.tanh(gates[:, 0:half] + 0.5 * br_ref[:, cols]) * half_scale + half_scale
        gi = 0.5 * jnp.tanh(gates[:, half:2 * half] + 0.5 * bi_ref[:, cols]) + 0.5
        a = jnp.exp(log_a)
        m2 = jnp.tanh(log_a) * (-1.0 - a * a)
        mult = jnp.where(m2 > 0.0, m2 * lax.rsqrt(m2), 0.0)
        a_scr[:, cols] = a
        b_scr[:, cols] = mult * (gi * xh)

    def mix_scan():
        h = carry_scr[...]
        for t in range(tt):
            r0 = t * SUBLANES
            h = a_scr[r0:r0 + SUBLANES, :] * h + b_scr[r0:r0 + SUBLANES, :]
            b_scr[r0:r0 + SUBLANES, :] = h
        carry_scr[...] = h

    def mix_gelu():
        hx = ga_scr[...]
        inner = hx * (hx * hx * (8.0 * GELU_K * GELU_C) + 2.0 * GELU_K)
        y_scr[:, 0:lw] = ((hx + hx * jnp.tanh(inner)) * b_scr[...]).astype(bf16)

    conv_rows = CONV_STEPS * SUBLANES
    groups = (CONV_STEPS, SUBLANES, LANES)

    def mix_conv(q):
        lanes = slice(q * LANES, (q + 1) * LANES)
        taps = [cw31_ref[k31 - 1 - s, :, lanes] for s in range(k31)]
        for c0 in range(0, rows, conv_rows):
            acc = jnp.broadcast_to(cb31_ref[:, lanes].reshape(1, 1, LANES), groups)
            for s in range(k31):
                off = halo31 - s * SUBLANES + c0
                acc = acc + taps[s][None] * vb_buf[off:off + conv_rows, lanes].reshape(groups)
            vc_scr[c0:c0 + conv_rows, lanes] = acc.reshape(conv_rows, LANES)

    def mix_head_norm(j):
        cols = slice(j * half, (j + 1) * half)
        vc = vc_scr[:, cols]
        dv = vc - _dot(vc.astype(bf16), mavg_ref[...])
        var = _dot((dv * dv).astype(bf16), mavg_ref[...])
        hn = dv * lax.rsqrt(var + LN_EPS) * (0.5 * ng_ref[:, cols]) + 0.5 * nb_ref[:, cols]
        y_scr[:, lw + j * half:lw + (j + 1) * half] = _silu_from_half(hn).astype(bf16)

    def mix_out_lru():
        y1_scr[...] = _dot(y_scr[:, 0:lw], w_out_ref[0:lw, :])

    def mix_out():
        part = _dot(y_scr[:, lw:], w_out_ref[lw:, :])
        gate = 1.0 + mod_ref[2]
        for r0 in range(0, rows, NORM_ROWS):
            rs = slice(r0, r0 + NORM_ROWS)
            x_rows = jnp.concatenate([x_tb[c, rs, :] for c in range(d // LANES)], axis=1)
            z = alpha * x_rows + _scale_shift(y1_scr[rs, :] + part[rs, :], gate)
            x1_scr[rs, :] = _layer_norm_rows(z, ln1g_ref[...], ln1b_ref[...])


    def ffn_in():
        x1 = x1_scr[...]
        u2_scr[...] = _scale_shift(x1, 1.0 + modp_ref[4], modp_ref[3]).astype(bf16)
        res2_scr[...] = alpha * x1

    def ffn_chunk(c0):
        cols = slice(c0, c0 + FFN_COLS)
        u2 = u2_scr[...]
        g = _dot(u2, w_up_ref[:, dff + c0:dff + c0 + FFN_COLS])
        g_ext = jnp.concatenate([g_hist[:, cols], g], axis=0)
        g_hist[:, cols] = g[rows - halof:, :]
        hg = 0.5 * cbf_ref[:, cols]
        for s in range(kc):
            off = halof - s * SUBLANES
            hg = hg + (0.5 * cwf_ref[kc - 1 - s:kc - s, cols]) * g_ext[off:off + rows, :]
        v = _dot(u2, w_up_ref[:, cols])
        act_scr[:, cols] = (_silu_from_half(hg) * v).astype(bf16)

    def ffn_down(k0, k1):
        part = _dot(act_scr[:, k0:k1], w_down_ref[k0:k1, :])
        if k0 == 0:
            y2_scr[...] = part
        else:
            y2_scr[...] += part

    def ffn_out(k0):
        part = _dot(act_scr[:, k0:], w_down_ref[k0:, :])
        gate = 1.0 + modp_ref[5]
        for r0 in range(0, rows, NORM_ROWS):
            rs = slice(r0, r0 + NORM_ROWS)
            z = res2_scr[rs, :] + _scale_shift(y2_scr[rs, :] + part[rs, :], gate)
            res = _layer_norm_rows(z, ln2g_ref[...], ln2b_ref[...])
            for c in range(d // LANES):
                res_tb[c, rs, :] = res[:, c * LANES:(c + 1) * LANES]

    def from_time_major():
        for b in range(SUBLANES):
            for c in range(d // LANES):
                o_ref[b, :, c * LANES:(c + 1) * LANES] = res_tb[c, pl.ds(b, tt, stride=SUBLANES), :]

    chunk = lambda n: functools.partial(ffn_chunk, n * FFN_COLS)
    down = lambda n0, n1: functools.partial(ffn_down, n0 * FFN_COLS, n1 * FFN_COLS)
    conv = lambda q: functools.partial(mix_conv, q)
    gates = lambda j: functools.partial(mix_gates, j)
    head = lambda j: functools.partial(mix_head_norm, j)
    assert dff == 11 * FFN_COLS and tt % CONV_STEPS == 0 and lw == 4 * LANES
    convs = lambda q: [conv(q)]

    schedule = [
        ffn_in, chunk(0),
        to_time_major, in_mod, in_proj_xa, chunk(1), in_proj_glu, chunk(2), in_proj_ga,
        gates(0), gates(1), chunk(3), *convs(0),
        mix_scan, mix_gelu, chunk(4), *convs(1),
        down(0, 3), chunk(5), *convs(2), mix_out_lru,
        chunk(6), *convs(3),
        chunk(7), down(3, 6), head(0),
        chunk(8), head(1),
        chunk(9), mix_out, move_history,
        down(6, 9), chunk(10),
        functools.partial(ffn_out, 9 * FFN_COLS), from_time_major,
    ]
    for piece in schedule:
        piece()


def _block(x, mod, mixer_consts, ffn_consts, *, alpha, tt):
    bsz, seq, d = x.shape
    (w_in, cw4, cb4, wg, br, bi, lam, cw31, cb31, ng, nb, mavg, w_out, ln1g, ln1b) = mixer_consts
    (w_up, cwf, cbf, w_down, ln2g, ln2b) = ffn_consts
    k4, lw = cw4.shape
    k31 = cw31.shape[0]
    kc, dff = cwf.shape
    nt = seq // tt
    n_blocks = (bsz // SUBLANES) * nt
    rows = tt * SUBLANES
    assert seq % tt == 0 and bsz % SUBLANES == 0 and d % LANES == 0 and dff % FFN_COLS == 0
    assert k4 - 1 <= tt and k31 - 1 <= tt and kc - 1 <= tt
    cw31 = jnp.broadcast_to(cw31[:, None, :], (k31, SUBLANES, lw))
    mod = mod.reshape(bsz, -1, d).transpose(1, 0, 2)
    consts = (w_in, cw4, cb4, wg, br, bi, lam, cw31, cb31, ng, nb, mavg, w_out, ln1g, ln1b,
              w_up, cwf, cbf, w_down, ln2g, ln2b)
    kern = functools.partial(_block_kernel, alpha=alpha, tt=tt, nt=nt, lw=lw, k4=k4, k31=k31,
                             dff=dff, kc=kc)

    def lagged(lag):
        return lambda i: jnp.clip(i - lag, 0, n_blocks - 1)

    def x_spec(lag):
        blk = lagged(lag)
        return pl.BlockSpec((SUBLANES, tt, d), lambda i: (blk(i) // nt, blk(i) % nt, 0))

    def mod_spec(lag):
        blk = lagged(lag)
        return pl.BlockSpec((mod.shape[0], SUBLANES, d), lambda i: (0, blk(i) // nt, 0))

    def const_spec(a):
        return pl.BlockSpec(a.shape, lambda i: (0,) * a.ndim, pipeline_mode=pl.Buffered(1))

    f32 = jnp.float32
    return pl.pallas_call(
        kern,
        grid=(n_blocks + 1,),
        in_specs=[x_spec(0), mod_spec(0), mod_spec(1)] + [const_spec(a) for a in consts],
        out_specs=x_spec(1),
        out_shape=jax.ShapeDtypeStruct(x.shape, f32),
        scratch_shapes=[
            pltpu.VMEM((d // LANES, rows, LANES), f32),
            pltpu.VMEM((rows, d), jnp.bfloat16),
            pltpu.VMEM(((k4 - 1) * SUBLANES + rows, lw), f32),
            pltpu.VMEM((rows, lw), f32),
            pltpu.VMEM(((k31 - 1) * SUBLANES + rows, lw), f32),
            pltpu.VMEM((rows, lw), f32),
            pltpu.VMEM((rows, lw), f32),
            pltpu.VMEM((rows, lw), f32),
            pltpu.VMEM((rows, d), jnp.bfloat16),
            pltpu.VMEM((SUBLANES, lw), f32),
            pltpu.VMEM((rows, d), f32),
            pltpu.VMEM((rows, d), jnp.bfloat16),
            pltpu.VMEM(((kc - 1) * SUBLANES, dff), f32),
            pltpu.VMEM((rows, dff), jnp.bfloat16),
            pltpu.VMEM((rows, d), f32),
            pltpu.VMEM((rows, d), f32),
            pltpu.VMEM((d // LANES, rows, LANES), f32),
            pltpu.VMEM((rows, d), f32),
        ],
        compiler_params=pltpu.CompilerParams(
            dimension_semantics=("arbitrary",), vmem_limit_bytes=VMEM_LIMIT_BYTES),
        name="mixer_ffn_block",
    )(x, mod, mod, *consts)


def _block_diag_tiles(w, tile):
    h, dh, _ = w.shape
    per = tile // dh
    w = w.reshape(h // per, per, dh, dh)
    eye = jnp.eye(per, dtype=w.dtype)
    return jnp.einsum("gpij,pq->gpiqj", w, eye).reshape(h // per, tile, tile)


def kernel(x, c, w_ada, b_ada, w_in, lru_conv_w, lru_conv_b, lru_w_r, lru_b_r, lru_w_i, lru_b_i,
           lru_lambda, conv_w, conv_b, conv_norm_g, conv_norm_b, w_out, ln1_g, ln1_b, ffn_w_up,
           ffn_conv_w, ffn_conv_b, ffn_w_down, ln2_g, ln2_b):
    depth = w_ada.shape[0]
    alpha = (2 * depth) ** 0.25
    bf16 = jnp.bfloat16
    lw = lru_conv_w.shape[-1]
    conv_heads = lru_w_r.shape[1]
    tile = lw // 2
    conv_head_dim = conv_w.shape[-1] // conv_heads
    mavg = jnp.kron(jnp.eye(tile // conv_head_dim, dtype=jnp.float32),
                    jnp.full((conv_head_dim, conv_head_dim), 1.0 / conv_head_dim, jnp.float32)).astype(bf16)
    row = lambda v: v.reshape(1, -1)

    for l in range(depth):
        mod = _modulation(c, w_ada[l], b_ada[l])
        wg = (0.5 * jnp.concatenate([_block_diag_tiles(lru_w_r[l], tile),
                                     _block_diag_tiles(lru_w_i[l], tile)], axis=-1)).astype(bf16)
        in_scale = jnp.concatenate([jnp.ones((lw,), jnp.float32), jnp.full((3 * lw,), 0.5, jnp.float32)])
        mixer_consts = ((w_in[l] * in_scale).astype(bf16), lru_conv_w[l], row(lru_conv_b[l]), wg, row(lru_b_r[l]),
                        row(lru_b_i[l]), row(lru_lambda[l]), conv_w[l], row(conv_b[l]),
                        row(conv_norm_g[l]), row(conv_norm_b[l]), mavg, w_out[l].astype(bf16),
                        row(ln1_g[l]), row(ln1_b[l]))
        ffn_consts = (ffn_w_up[l].astype(bf16), ffn_conv_w[l], row(ffn_conv_b[l]),
                      ffn_w_down[l].astype(bf16), row(ln2_g[l]), row(ln2_b[l]))
        x = _block(x, mod, mixer_consts, ffn_consts, alpha=alpha, tt=TIME_STEPS)
    return x
```

```python
import functools

import jax
import jax.numpy as jnp
from jax import lax
from jax.experimental import pallas as pl
from jax.experimental.pallas import tpu as pltpu

LRU_C = 8.0
LN_EPS = 1e-5
SUBLANES = 8
LANES = 128
TIME_STEPS = 32
FFN_COLS = 256
CONV_STEPS = 32
NORM_ROWS = 64
VMEM_LIMIT_BYTES = 58 * 1024 * 1024
GELU_K = 0.7978845608028654
GELU_C = 0.044715


def _dot(a, b):
    return jnp.dot(a, b, preferred_element_type=jnp.float32)


def _sigmoid(v):
    return 0.5 * jnp.tanh(0.5 * v) + 0.5


def _silu_from_half(h):
    return h + h * jnp.tanh(h)


def _layer_norm_rows(z, g, b):
    mu = jnp.mean(z, axis=-1, keepdims=True)
    d = z - mu
    var = jnp.mean(d * d, axis=-1, keepdims=True)
    return d * lax.rsqrt(var + LN_EPS) * g + b


def _scale_shift(v, scale_tile, shift_tile=None):
    v3 = v.reshape(v.shape[0] // SUBLANES, SUBLANES, v.shape[1]) * scale_tile[None]
    if shift_tile is not None:
        v3 = v3 + shift_tile[None]
    return v3.reshape(v.shape)


def _mod_kernel(c_ref, w_ref, b_ref, o_ref):
    c = c_ref[...]
    c_act = c * _sigmoid(c)
    o_ref[...] = _dot(c_act, w_ref[...]) + b_ref[...]


def _modulation(c, w_ada, b_ada):
    bsz, d = c.shape
    n = w_ada.shape[1]
    return pl.pallas_call(
        _mod_kernel,
        grid=(n // d,),
        in_specs=[
            pl.BlockSpec((bsz, d), lambda j: (0, 0)),
            pl.BlockSpec((d, d), lambda j: (0, j)),
            pl.BlockSpec((1, d), lambda j: (0, j)),
        ],
        out_specs=pl.BlockSpec((bsz, d), lambda j: (0, j)),
        out_shape=jax.ShapeDtypeStruct((bsz, n), jnp.float32),
        compiler_params=pltpu.CompilerParams(
            dimension_semantics=("arbitrary",), vmem_limit_bytes=VMEM_LIMIT_BYTES),
        name="adaln_modulation",
    )(c, w_ada, b_ada.reshape(1, n))


def _block_kernel(x_ref, mod_ref, modp_ref,
                  w_in_ref, cw4_ref, cb4_ref, wg_ref, br_ref, bi_ref, lam_ref,
                  cw31_ref, cb31_ref, ng_ref, nb_ref, mavg_ref, w_out_ref, ln1g_ref, ln1b_ref,
                  w_up_ref, cwf_ref, cbf_ref, w_down_ref, ln2g_ref, ln2b_ref,
                  o_ref,
                  x_tb, u_scr, xa_buf, ga_scr, vb_buf, a_scr, b_scr, vc_scr, y_scr,
                  carry_scr, x1_scr, u2_scr, g_hist, act_scr, y1_scr, y2_scr, res_tb,
                  res2_scr,
                  *, alpha, tt, nt, lw, k4, k31, dff, kc):
    i = pl.program_id(0)
    t_mix = lax.rem(i, nt)
    t_ffn = lax.rem(i + nt - 1, nt)
    rows = tt * SUBLANES
    d = x1_scr.shape[1]
    half = lw // 2
    bf16 = jnp.bfloat16
    halo4 = (k4 - 1) * SUBLANES
    halo31 = (k31 - 1) * SUBLANES
    halof = (kc - 1) * SUBLANES

    @pl.when(i == 0)
    def _():
        x1_scr[...] = jnp.zeros_like(x1_scr)

    @pl.when(t_mix == 0)
    def _():
        xa_buf[0:halo4, :] = jnp.zeros((halo4, lw), jnp.float32)
        vb_buf[0:halo31, :] = jnp.zeros((halo31, lw), jnp.float32)
        carry_scr[...] = jnp.zeros_like(carry_scr)

    @pl.when(jnp.logical_or(t_ffn == 0, i == 0))
    def _():
        g_hist[...] = jnp.zeros_like(g_hist)

    def time_major(ref3):
        return jnp.concatenate([ref3[c] for c in range(d // LANES)], axis=1)


    def to_time_major():
        for b in range(SUBLANES):
            for c in range(d // LANES):
                x_tb[c, pl.ds(b, tt, stride=SUBLANES), :] = x_ref[b, :, c * LANES:(c + 1) * LANES]

    def in_mod():
        u_scr[...] = _scale_shift(time_major(x_tb), 1.0 + mod_ref[1], mod_ref[0]).astype(bf16)

    def in_proj_xa():
        xa_buf[halo4:halo4 + rows, :] = _dot(u_scr[...], w_in_ref[:, 0:lw])

    def in_proj_ga():
        ga_scr[...] = _dot(u_scr[...], w_in_ref[:, lw:2 * lw])

    def in_proj_glu():
        u = u_scr[...]
        hv = _dot(u, w_in_ref[:, 2 * lw:3 * lw])
        hg = _dot(u, w_in_ref[:, 3 * lw:4 * lw])
        vb_buf[halo31:halo31 + rows, :] = hv + hv * jnp.tanh(hg)

    def move_history():
        xa_buf[0:halo4, :] = xa_buf[rows:rows + halo4, :]
        vb_buf[0:halo31, :] = vb_buf[rows:rows + halo31, :]

    def mix_gates(j):
        cols = slice(j * half, (j + 1) * half)
        xh = cb4_ref[:, cols]
        for s in range(k4):
            off = halo4 - s * SUBLANES
            xh = xh + cw4_ref[k4 - 1 - s:k4 - s, cols] * xa_buf[off:off + rows, cols]
        gates = _dot(xh.astype(bf16), wg_ref[j])
        half_scale = -0.5 * LRU_C * jax.nn.softplus(-lam_ref[:, cols])
        log_a = jnp.tanh(gates[:, 0:half] + 0.5 * br_ref[:, cols]) * half_scale + half_scale
        gi = 0.5 * jnp.tanh(gates[:, half:2 * half] + 0.5 * bi_ref[:, cols]) + 0.5
        a = jnp.exp(log_a)
        m2 = jnp.tanh(log_a) * (-1.0 - a * a)
        mult = jnp.where(m2 > 0.0, m2 * lax.rsqrt(m2), 0.0)
        a_scr[:, cols] = a
        b_scr[:, cols] = mult * (gi * xh)

    def mix_scan():
        h = carry_scr[...]
        for t in range(tt):
            r0 = t * SUBLANES
            h = a_scr[r0:r0 + SUBLANES, :] * h + b_scr[r0:r0 + SUBLANES, :]
            b_scr[r0:r0 + SUBLANES, :] = h
        carry_scr[...] = h

    def mix_gelu():
        hx = ga_scr[...]
        inner = hx * (hx * hx * (8.0 * GELU_K * GELU_C) + 2.0 * GELU_K)
        y_scr[:, 0:lw] = ((hx + hx * jnp.tanh(inner)) * b_scr[...]).astype(bf16)

    conv_rows = CONV_STEPS * SUBLANES
    groups = (CONV_STEPS, SUBLANES, LANES)

    def mix_conv(q):
        lanes = slice(q * LANES, (q + 1) * LANES)
        taps = [cw31_ref[k31 - 1 - s, :, lanes] for s in range(k31)]
        for c0 in range(0, rows, conv_rows):
            acc = jnp.broadcast_to(cb31_ref[:, lanes].reshape(1, 1, LANES), groups)
            for s in range(k31):
                off = halo31 - s * SUBLANES + c0
                acc = acc + taps[s][None] * vb_buf[off:off + conv_rows, lanes].reshape(groups)
            vc_scr[c0:c0 + conv_rows, lanes] = acc.reshape(conv_rows, LANES)

    def mix_head_norm(j):
        cols = slice(j * half, (j + 1) * half)
        vc = vc_scr[:, cols]
        dv = vc - _dot(vc.astype(bf16), mavg_ref[...])
        var = _dot((dv * dv).astype(bf16), mavg_ref[...])
        hn = dv * lax.rsqrt(var + LN_EPS) * (0.5 * ng_ref[:, cols]) + 0.5 * nb_ref[:, cols]
        y_scr[:, lw + j * half:lw + (j + 1) * half] = _silu_from_half(hn).astype(bf16)

    def mix_out_lru():
        y1_scr[...] = _dot(y_scr[:, 0:lw], w_out_ref[0:lw, :])

    def mix_out():
        part = _dot(y_scr[:, lw:], w_out_ref[lw:, :])
        gate = 1.0 + mod_ref[2]
        for r0 in range(0, rows, NORM_ROWS):
            rs = slice(r0, r0 + NORM_ROWS)
            x_rows = jnp.concatenate([x_tb[c, rs, :] for c in range(d // LANES)], axis=1)
            z = alpha * x_rows + _scale_shift(y1_scr[rs, :] + part[rs, :], gate)
            x1_scr[rs, :] = _layer_norm_rows(z, ln1g_ref[...], ln1b_ref[...])


    def ffn_in():
        x1 = x1_scr[...]
        u2_scr[...] = _scale_shift(x1, 1.0 + modp_ref[4], modp_ref[3]).astype(bf16)
        res2_scr[...] = alpha * x1

    def ffn_chunk(c0):
        cols = slice(c0, c0 + FFN_COLS)
        u2 = u2_scr[...]
        g = _dot(u2, w_up_ref[:, dff + c0:dff + c0 + FFN_COLS])
        g_ext = jnp.concatenate([g_hist[:, cols], g], axis=0)
        g_hist[:, cols] = g[rows - halof:, :]
        hg = 0.5 * cbf_ref[:, cols]
        for s in range(kc):
            off = halof - s * SUBLANES
            hg = hg + (0.5 * cwf_ref[kc - 1 - s:kc - s, cols]) * g_ext[off:off + rows, :]
        v = _dot(u2, w_up_ref[:, cols])
        hb = hg.astype(bf16)
        act_scr[:, cols] = _silu_from_half(hb) * v.astype(bf16)

    def ffn_down(k0, k1):
        part = _dot(act_scr[:, k0:k1], w_down_ref[k0:k1, :])
        if k0 == 0:
            y2_scr[...] = part
        else:
            y2_scr[...] += part

    def ffn_out(k0):
        part = _dot(act_scr[:, k0:], w_down_ref[k0:, :])
        gate = 1.0 + modp_ref[5]
        for r0 in range(0, rows, NORM_ROWS):
            rs = slice(r0, r0 + NORM_ROWS)
            z = res2_scr[rs, :] + _scale_shift(y2_scr[rs, :] + part[rs, :], gate)
            res = _layer_norm_rows(z, ln2g_ref[...], ln2b_ref[...])
            for c in range(d // LANES):
                res_tb[c, rs, :] = res[:, c * LANES:(c + 1) * LANES]

    def from_time_major():
        for b in range(SUBLANES):
            for c in range(d // LANES):
                o_ref[b, :, c * LANES:(c + 1) * LANES] = res_tb[c, pl.ds(b, tt, stride=SUBLANES), :]

    chunk = lambda n: functools.partial(ffn_chunk, n * FFN_COLS)
    down = lambda n0, n1: functools.partial(ffn_down, n0 * FFN_COLS, n1 * FFN_COLS)
    conv = lambda q: functools.partial(mix_conv, q)
    gates = lambda j: functools.partial(mix_gates, j)
    head = lambda j: functools.partial(mix_head_norm, j)
    assert dff == 11 * FFN_COLS and tt % CONV_STEPS == 0 and lw == 4 * LANES
    convs = lambda q: [conv(q)]

    schedule = [
        ffn_in, chunk(0),
        to_time_major, in_mod, in_proj_xa, chunk(1), in_proj_glu, chunk(2), in_proj_ga,
        gates(0), gates(1), chunk(3), *convs(0),
        mix_scan, mix_gelu, chunk(4), *convs(1),
        down(0, 3), chunk(5), *convs(2), mix_out_lru,
        chunk(6), *convs(3),
        chunk(7), down(3, 6), head(0),
        chunk(8), head(1),
        chunk(9), mix_out, move_history,
        down(6, 9), chunk(10),
        functools.partial(ffn_out, 9 * FFN_COLS), from_time_major,
    ]
    for piece in schedule:
        piece()


def _block(x, mod, mixer_consts, ffn_consts, *, alpha, tt):
    bsz, seq, d = x.shape
    (w_in, cw4, cb4, wg, br, bi, lam, cw31, cb31, ng, nb, mavg, w_out, ln1g, ln1b) = mixer_consts
    (w_up, cwf, cbf, w_down, ln2g, ln2b) = ffn_consts
    k4, lw = cw4.shape
    k31 = cw31.shape[0]
    kc, dff = cwf.shape
    nt = seq // tt
    n_blocks = (bsz // SUBLANES) * nt
    rows = tt * SUBLANES
    assert seq % tt == 0 and bsz % SUBLANES == 0 and d % LANES == 0 and dff % FFN_COLS == 0
    assert k4 - 1 <= tt and k31 - 1 <= tt and kc - 1 <= tt
    cw31 = jnp.broadcast_to(cw31[:, None, :], (k31, SUBLANES, lw))
    mod = mod.reshape(bsz, -1, d).transpose(1, 0, 2)
    consts = (w_in, cw4, cb4, wg, br, bi, lam, cw31, cb31, ng, nb, mavg, w_out, ln1g, ln1b,
              w_up, cwf, cbf, w_down, ln2g, ln2b)
    kern = functools.partial(_block_kernel, alpha=alpha, tt=tt, nt=nt, lw=lw, k4=k4, k31=k31,
                             dff=dff, kc=kc)

    def lagged(lag):
        return lambda i: jnp.clip(i - lag, 0, n_blocks - 1)

    def x_spec(lag):
        blk = lagged(lag)
        return pl.BlockSpec((SUBLANES, tt, d), lambda i: (blk(i) // nt, blk(i) % nt, 0))

    def mod_spec(lag):
        blk = lagged(lag)
        return pl.BlockSpec((mod.shape[0], SUBLANES, d), lambda i: (0, blk(i) // nt, 0))

    def const_spec(a):
        return pl.BlockSpec(a.shape, lambda i: (0,) * a.ndim, pipeline_mode=pl.Buffered(1))

    f32 = jnp.float32
    return pl.pallas_call(
        kern,
        grid=(n_blocks + 1,),
        in_specs=[x_spec(0), mod_spec(0), mod_spec(1)] + [const_spec(a) for a in consts],
        out_specs=x_spec(1),
        out_shape=jax.ShapeDtypeStruct(x.shape, f32),
        scratch_shapes=[
            pltpu.VMEM((d // LANES, rows, LANES), f32),
            pltpu.VMEM((rows, d), jnp.bfloat16),
            pltpu.VMEM(((k4 - 1) * SUBLANES + rows, lw), f32),
            pltpu.VMEM((rows, lw), f32),
            pltpu.VMEM(((k31 - 1) * SUBLANES + rows, lw), f32),
            pltpu.VMEM((rows, lw), f32),
            pltpu.VMEM((rows, lw), f32),
            pltpu.VMEM((rows, lw), f32),
            pltpu.VMEM((rows, d), jnp.bfloat16),
            pltpu.VMEM((SUBLANES, lw), f32),
            pltpu.VMEM((rows, d), f32),
            pltpu.VMEM((rows, d), jnp.bfloat16),
            pltpu.VMEM(((kc - 1) * SUBLANES, dff), f32),
            pltpu.VMEM((rows, dff), jnp.bfloat16),
            pltpu.VMEM((rows, d), f32),
            pltpu.VMEM((rows, d), f32),
            pltpu.VMEM((d // LANES, rows, LANES), f32),
            pltpu.VMEM((rows, d), f32),
        ],
        compiler_params=pltpu.CompilerParams(
            dimension_semantics=("arbitrary",), vmem_limit_bytes=VMEM_LIMIT_BYTES),
        name="mixer_ffn_block",
    )(x, mod, mod, *consts)


def _block_diag_tiles(w, tile):
    h, dh, _ = w.shape
    per = tile // dh
    w = w.reshape(h // per, per, dh, dh)
    eye = jnp.eye(per, dtype=w.dtype)
    return jnp.einsum("gpij,pq->gpiqj", w, eye).reshape(h // per, tile, tile)


def kernel(x, c, w_ada, b_ada, w_in, lru_conv_w, lru_conv_b, lru_w_r, lru_b_r, lru_w_i, lru_b_i,
           lru_lambda, conv_w, conv_b, conv_norm_g, conv_norm_b, w_out, ln1_g, ln1_b, ffn_w_up,
           ffn_conv_w, ffn_conv_b, ffn_w_down, ln2_g, ln2_b):
    depth = w_ada.shape[0]
    alpha = (2 * depth) ** 0.25
    bf16 = jnp.bfloat16
    lw = lru_conv_w.shape[-1]
    conv_heads = lru_w_r.shape[1]
    tile = lw // 2
    conv_head_dim = conv_w.shape[-1] // conv_heads
    mavg = jnp.kron(jnp.eye(tile // conv_head_dim, dtype=jnp.float32),
                    jnp.full((conv_head_dim, conv_head_dim), 1.0 / conv_head_dim, jnp.float32)).astype(bf16)
    row = lambda v: v.reshape(1, -1)

    for l in range(depth):
        mod = _modulation(c, w_ada[l], b_ada[l])
        wg = (0.5 * jnp.concatenate([_block_diag_tiles(lru_w_r[l], tile),
                                     _block_diag_tiles(lru_w_i[l], tile)], axis=-1)).astype(bf16)
        in_scale = jnp.concatenate([jnp.ones((lw,), jnp.float32), jnp.full((3 * lw,), 0.5, jnp.float32)])
        mixer_consts = ((w_in[l] * in_scale).astype(bf16), lru_conv_w[l], row(lru_conv_b[l]), wg, row(lru_b_r[l]),
                        row(lru_b_i[l]), row(lru_lambda[l]), conv_w[l], row(conv_b[l]),
                        row(conv_norm_g[l]), row(conv_norm_b[l]), mavg, w_out[l].astype(bf16),
                        row(ln1_g[l]), row(ln1_b[l]))
        ffn_consts = (ffn_w_up[l].astype(bf16), ffn_conv_w[l], row(ffn_conv_b[l]),
                      ffn_w_down[l].astype(bf16), row(ln2_g[l]), row(ln2_b[l]))
        x = _block(x, mod, mixer_consts, ffn_consts, alpha=alpha, tt=TIME_STEPS)
    return x
```
